```python
import math
import jax, jax.numpy as jnp
from jax import lax
import numpy as np

D_MODEL = 1024
BATCH = 8
SEQ = 4096
DEPTH = 2

N_MIXERS = 2
N_META = 16
POOL_WINDOWS = (2, 4, 8, 16)
N_POOL_GROUPS = len(POOL_WINDOWS)
POOL_GC = D_MODEL // N_POOL_GROUPS
HEAD_DIM = 64
N_HEADS = D_MODEL // (2 * HEAD_DIM)
V_DIM = 2 * HEAD_DIM
Q_BLOCK = 128
N_BUCKETS = 32
MAX_DISTANCE = 128
D_FF = ((8 * D_MODEL + 3 * 256 - 1) // (3 * 256)) * 256
N_POOL_LAYERS = (DEPTH + 1) // 2
N_ATTN_LAYERS = DEPTH // 2
RMS_EPS = 1e-6
NEG_INF = -1e30

kernel_name = "hybrid_pool_diffattn_swiglu"


def rms_norm(x, g):
    xf = x.astype(jnp.float32)
    y = xf * lax.rsqrt(jnp.mean(xf * xf, axis=-1, keepdims=True) + RMS_EPS)
    return (y * g.astype(jnp.float32)).astype(x.dtype)


def pool_mixer(h, w_groups, scale):
    B, L, D = h.shape
    hf = h.astype(jnp.float32)
    c0 = jnp.concatenate([jnp.zeros((B, 1, D), jnp.float32), jnp.cumsum(hf, axis=1)], axis=1)
    t = jnp.arange(L)
    outs = []
    for g, w in enumerate(POOL_WINDOWS):
        sl = slice(g * POOL_GC, (g + 1) * POOL_GC)
        cg = c0[:, :, sl]
        lead = jnp.concatenate([jnp.zeros((B, w - 1, POOL_GC), jnp.float32), cg[:, : L + 1 - w]], axis=1)
        cnt = jnp.minimum(t + 1, w).astype(jnp.float32)[None, :, None]
        pooled = (cg[:, 1:] - lead) / cnt - hf[:, :, sl]
        outs.append(pooled.astype(h.dtype) @ w_groups[g])
    return jnp.concatenate(outs, axis=-1) * scale


def t5_causal_bucket(rel):
    n = jnp.maximum(rel, 0)
    max_exact = N_BUCKETS // 2
    nf = jnp.maximum(n, max_exact).astype(jnp.float32)
    large = max_exact + (jnp.log(nf / max_exact) / math.log(MAX_DISTANCE / max_exact)
                         * (N_BUCKETS - max_exact)).astype(jnp.int32)
    large = jnp.minimum(large, N_BUCKETS - 1)
    return jnp.where(n < max_exact, n, large)


def diff_attention(h, w_qkv, w_o, lq1, lk1, lq2, lk2, subln_g, rel_bias, lambda_init):
    B, L, D = h.shape
    Lp = ((L + Q_BLOCK - 1) // Q_BLOCK) * Q_BLOCK
    pad = ((0, 0), (0, Lp - L), (0, 0))
    qkv = h @ w_qkv
    q, k, v = jnp.split(qkv, 3, axis=-1)
    q = jnp.pad(q, pad).reshape(B, Lp, N_HEADS, 2, HEAD_DIM).transpose(0, 2, 3, 1, 4)
    k = jnp.pad(k, pad).reshape(B, Lp, N_HEADS, 2, HEAD_DIM).transpose(0, 2, 3, 1, 4)
    v = jnp.pad(v, pad).reshape(B, Lp, N_HEADS, V_DIM).transpose(0, 2, 1, 3).astype(jnp.float32)
    lam = (jnp.exp(jnp.sum(lq1.astype(jnp.float32) * lk1.astype(jnp.float32)))
           - jnp.exp(jnp.sum(lq2.astype(jnp.float32) * lk2.astype(jnp.float32))) + lambda_init)
    scale = 1.0 / math.sqrt(HEAD_DIM)
    kpos = jnp.arange(Lp)

    def block(i):
        q0 = i * Q_BLOCK
        qb = lax.dynamic_slice_in_dim(q, q0, Q_BLOCK, axis=3)
        rel = (q0 + jnp.arange(Q_BLOCK))[:, None] - kpos[None, :]
        bias = rel_bias[t5_causal_bucket(rel)].astype(jnp.float32).transpose(2, 0, 1)
        s = jnp.einsum('bhmqd,bhmkd->bhmqk', qb, k).astype(jnp.float32) * scale + bias[None, :, None]
        s = jnp.where(rel >= 0, s, NEG_INF)
        p = jax.nn.softmax(s, axis=-1)
        a = p[:, :, 0] - lam * p[:, :, 1]
        return jnp.einsum('bhqk,bhkv->bhqv', a, v)

    o = lax.map(block, jnp.arange(Lp // Q_BLOCK))
    o = o.transpose(1, 0, 3, 2, 4).reshape(B, Lp, N_HEADS, V_DIM)[:, :L]
    o = rms_norm(o, subln_g) * (1.0 - lambda_init)
    return o.reshape(B, L, D).astype(h.dtype) @ w_o


def swiglu(h, w_gate, w_up, w_down):
    return (jax.nn.silu(h @ w_gate) * (h @ w_up)) @ w_down


def setup_inputs(seed: int = 0) -> dict:
    key = jax.random.key(seed)
    ks = jax.random.split(key, 20)
    f32 = jnp.float32
    nrm = lambda k, s, sc: jax.random.normal(k, s, f32) * sc
    D, F = D_MODEL, D_FF
    return {
        "x": nrm(ks[0], (BATCH, SEQ, D), 1.0),
        "meta_tokens": nrm(ks[1], (N_META, D), 1.0),
        "rel_bias": nrm(ks[2], (N_BUCKETS, N_HEADS), 0.5),
        "mix_norm_g": 1.0 + nrm(ks[3], (DEPTH, D), 0.05),
        "ffn_norm_g": 1.0 + nrm(ks[4], (DEPTH, D), 0.05),
        "pool_w": nrm(ks[5], (N_POOL_LAYERS, N_POOL_GROUPS, POOL_GC, POOL_GC), POOL_GC ** -0.5),
        "pool_scale": 1.0 + nrm(ks[6], (N_POOL_LAYERS, D), 0.1),
        "attn_w_qkv": nrm(ks[7], (N_ATTN_LAYERS, D, 3 * D), D ** -0.5),
        "attn_w_o": nrm(ks[8], (N_ATTN_LAYERS, D, D), D ** -0.5),
        "lambda_q1": nrm(ks[9], (N_ATTN_LAYERS, HEAD_DIM), 0.1),
        "lambda_k1": nrm(ks[10], (N_ATTN_LAYERS, HEAD_DIM), 0.1),
        "lambda_q2": nrm(ks[11], (N_ATTN_LAYERS, HEAD_DIM), 0.1),
        "lambda_k2": nrm(ks[12], (N_ATTN_LAYERS, HEAD_DIM), 0.1),
        "subln_g": 1.0 + nrm(ks[13], (N_ATTN_LAYERS, V_DIM), 0.05),
        "ffn_w_gate": nrm(ks[14], (DEPTH, D, F), D ** -0.5),
        "ffn_w_up": nrm(ks[15], (DEPTH, D, F), D ** -0.5),
        "ffn_w_down": nrm(ks[16], (DEPTH, F, D), F ** -0.5),
        "final_norm_g": 1.0 + nrm(ks[17], (D,), 0.05),
    }


def reference(x, meta_tokens, rel_bias, mix_norm_g, ffn_norm_g, pool_w, pool_scale,
              attn_w_qkv, attn_w_o, lambda_q1, lambda_k1, lambda_q2, lambda_k2, subln_g,
              ffn_w_gate, ffn_w_up, ffn_w_down, final_norm_g):
    B = x.shape[0]
    meta = jnp.broadcast_to(meta_tokens.astype(x.dtype)[None], (B, N_META, D_MODEL))
    h = jnp.concatenate([meta, x], axis=1)
    for i in range(DEPTH):
        hn = rms_norm(h, mix_norm_g[i])
        j = i // N_MIXERS
        if i % N_MIXERS == 0:
            h = h + pool_mixer(hn, pool_w[j], pool_scale[j])
        else:
            lambda_init = 0.8 - 0.6 * math.exp(-0.3 * i)
            h = h + diff_attention(hn, attn_w_qkv[j], attn_w_o[j], lambda_q1[j], lambda_k1[j],
                                   lambda_q2[j], lambda_k2[j], subln_g[j], rel_bias, lambda_init)
        h = h + swiglu(rms_norm(h, ffn_norm_g[i]), ffn_w_gate[i], ffn_w_up[i], ffn_w_down[i])
    h = rms_norm(h, final_norm_g)
    return h[:, N_META:]
```

```python
import functools
import math

import numpy as np
import jax
import jax.numpy as jnp
from jax import lax
from jax.experimental import pallas as pl
from jax.experimental.pallas import tpu as pltpu

N_META = 16
POOL_WINDOWS = (2, 4, 8, 16)
HEAD_DIM = 64
V_DIM = 2 * HEAD_DIM
N_BUCKETS = 32
MAX_DISTANCE = 128
RMS_EPS = 1e-6
NEG_INF = -1e30
HALO = 16
V7X_VMEM_LIMIT = 56 * 1024 * 1024

F32 = jnp.float32
BF16 = jnp.bfloat16
_NT = (((1,), (1,)), ((), ()))


def _rms(x, g):
    ms = jnp.mean(x * x, axis=-1, keepdims=True)
    return x * lax.rsqrt(ms + RMS_EPS) * g


def _swiglu(hn, wg_ref, wu_ref, wd_ref, fc):
    acc = None
    for c in range(wg_ref.shape[1] // fc):
        sl = slice(c * fc, (c + 1) * fc)
        g = jnp.dot(hn, wg_ref[:, sl], preferred_element_type=F32)
        u = jnp.dot(hn, wu_ref[:, sl], preferred_element_type=F32)
        a = (g * jax.nn.sigmoid(g) * u).astype(BF16)
        d = jnp.dot(a, wd_ref[sl, :], preferred_element_type=F32)
        acc = d if acc is None else acc + d
    return acc


def _layer0_kernel(x_ref, xprev_ref, first_ref, g1_ref, pw_ref, ps_ref, g2_ref,
                   wg_ref, wu_ref, wd_ref, o_ref, *, tiles_per_seq, pos0, fc):
    i = pl.program_id(0)
    rows = x_ref.shape[0]
    gc = pw_ref.shape[1]
    tile_in_seq = i % tiles_per_seq
    x = x_ref[...]
    g1 = g1_ref[...]
    halo_raw = jnp.where(tile_in_seq == 0, first_ref[...], xprev_ref[...])
    hn = _rms(x, g1)
    full = jnp.concatenate([_rms(halo_raw, g1), hn], axis=0)
    pos = lax.broadcasted_iota(jnp.int32, (rows, 1), 0) + tile_in_seq * rows + pos0
    mixes = []
    for g, w in enumerate(POOL_WINDOWS):
        cols = slice(g * gc, (g + 1) * gc)
        s = full[:, cols]
        k = 1
        while k < w:
            s = s + pltpu.roll(s, k, 0)
            k *= 2
        inv_cnt = 1.0 / jnp.minimum(pos + 1, w).astype(F32)
        pooled = s[HALO:] * inv_cnt - hn[:, cols]
        mixes.append(jnp.dot(pooled.astype(BF16), pw_ref[g], preferred_element_type=F32))
    hmid = x + jnp.concatenate(mixes, axis=-1) * ps_ref[...]
    hn2 = _rms(hmid, g2_ref[...]).astype(BF16)
    o_ref[...] = hmid + _swiglu(hn2, wg_ref, wu_ref, wd_ref, fc)


def _resident(shape):
    return pl.BlockSpec(shape, lambda *_: (0,) * len(shape), pipeline_mode=pl.Buffered(1))


def _layer0(x, first, g1, pw, ps, g2, wg, wu, wd, *, rows, tiles_per_seq, pos0, fc):
    n, d = x.shape
    f = wg.shape[1]
    halo_blocks = rows // HALO
    kern = functools.partial(_layer0_kernel, tiles_per_seq=tiles_per_seq, pos0=pos0, fc=fc)
    return pl.pallas_call(
        kern,
        out_shape=jax.ShapeDtypeStruct((n, d), F32),
        grid=(n // rows,),
        in_specs=[
            pl.BlockSpec((rows, d), lambda i: (i, 0)),
            pl.BlockSpec((HALO, d), lambda i: (jnp.maximum(i * halo_blocks - 1, 0), 0)),
            _resident((HALO, d)),
            _resident((1, d)),
            _resident(pw.shape),
            _resident((1, d)),
            _resident((1, d)),
            _resident((d, f)),
            _resident((d, f)),
            _resident((f, d)),
        ],
        out_specs=pl.BlockSpec((rows, d), lambda i: (i, 0)),
        compiler_params=pltpu.CompilerParams(
            dimension_semantics=("arbitrary",), vmem_limit_bytes=V7X_VMEM_LIMIT),
        name="layer0_pool_swiglu",
    )(x, x, first, g1, pw, ps, g2, wg, wu, wd)


def _qkv_kernel(h_ref, g_ref, w_ref, o_ref):
    d = h_ref.shape[1]
    hn = _rms(h_ref[...], g_ref[...]).astype(BF16)
    for c in range(3):
        sl = slice(c * d, (c + 1) * d)
        y = jnp.dot(hn, w_ref[:, sl], preferred_element_type=F32)
        if c == 0:
            y = y * (1.0 / math.sqrt(HEAD_DIM))
        o_ref[:, sl] = y.astype(BF16)


def _qkv(h, g, w, *, rows):
    n, d = h.shape
    return pl.pallas_call(
        _qkv_kernel,
        out_shape=jax.ShapeDtypeStruct((n, 3 * d), BF16),
        grid=(n // rows,),
        in_specs=[
            pl.BlockSpec((rows, d), lambda i: (i, 0)),
            _resident((1, d)),
            _resident((d, 3 * d)),
        ],
        out_specs=pl.BlockSpec((rows, 3 * d), lambda i: (i, 0)),
        compiler_params=pltpu.CompilerParams(
            dimension_semantics=("arbitrary",), vmem_limit_bytes=V7X_VMEM_LIMIT),
        name="qkv_proj",
    )(h, g, w)


def _bucket_table(rel):
    n = np.maximum(rel, 0)
    max_exact = N_BUCKETS // 2
    nf = np.maximum(n, max_exact).astype(np.float32)
    large = max_exact + (np.log(nf / np.float32(max_exact)) / np.float32(math.log(MAX_DISTANCE / max_exact))
                         * np.float32(N_BUCKETS - max_exact)).astype(np.int32)
    large = np.minimum(large, N_BUCKETS - 1)
    return np.where(rel < 0, -1, np.where(n < max_exact, n, large)).astype(np.int32)


def _bias_kernel(rb_ref, lv_ref, idxd_ref, idxm_ref, bd_ref, bm_ref, lam_ref, *, lambda_init):
    h = pl.program_id(0)

    def build(idx):
        out = jnp.full(idx.shape, NEG_INF, F32)
        for b in range(N_BUCKETS):
            out = jnp.where(idx == b, rb_ref[b, h], out)
        return out

    bd_ref[...] = build(idxd_ref[...])
    bm_ref[...] = build(idxm_ref[...])
    lv = lv_ref[...]
    e1 = jnp.exp(jnp.sum(lv[0:1] * lv[1:2], axis=-1, keepdims=True))
    e2 = jnp.exp(jnp.sum(lv[2:3] * lv[3:4], axis=-1, keepdims=True))
    lam_ref[...] = e1 - e2 + lambda_init


def _bias_tables(rel_bias, lam_vecs, *, tq, lambda_init):
    n_heads = rel_bias.shape[1]
    qi = np.arange(tq)[:, None]
    idx_d = np.stack([_bucket_table(d * tq + qi - np.arange(tq)[None, :]) for d in (0, 1)])
    idx_m = _bucket_table(N_META + qi - np.arange(N_META)[None, :])
    kern = functools.partial(_bias_kernel, lambda_init=lambda_init)
    return pl.pallas_call(
        kern,
        out_shape=(jax.ShapeDtypeStruct((n_heads, 2, tq, tq), F32),
                   jax.ShapeDtypeStruct((n_heads, tq, N_META), F32),
                   jax.ShapeDtypeStruct((1, 1), F32)),
        grid=(n_heads,),
        in_specs=[
            pl.BlockSpec(memory_space=pltpu.SMEM),
            pl.BlockSpec(lam_vecs.shape, lambda h: (0, 0)),
            pl.BlockSpec(idx_d.shape, lambda h: (0, 0, 0)),
            pl.BlockSpec(idx_m.shape, lambda h: (0, 0)),
        ],
        out_specs=(pl.BlockSpec((None, 2, tq, tq), lambda h: (h, 0, 0, 0)),
                   pl.BlockSpec((None, tq, N_META), lambda h: (h, 0, 0)),
                   pl.BlockSpec((1, 1), lambda h: (0, 0))),
        compiler_params=pltpu.CompilerParams(dimension_semantics=("arbitrary",)),
        name="rel_bias_tiles",
    )(rel_bias, lam_vecs, jnp.asarray(idx_d), jnp.asarray(idx_m))


def _attn_kernel(lam_ref, rb_ref, q_ref, k_ref, v_ref, km_ref, vm_ref, bd_ref, bm_ref, g_ref,
                 o_ref, *, lambda_init):
    h = pl.program_id(1)
    i = pl.program_id(2)
    tq = q_ref.shape[0]
    q = q_ref[...]
    lane = lax.broadcasted_iota(jnp.int32, q.shape, 1)
    zero = jnp.zeros_like(q)
    qs = jnp.concatenate([jnp.where(lane < HEAD_DIM, q, zero),
                          jnp.where(lane >= HEAD_DIM, q, zero)], axis=0)
    far_bias = rb_ref[N_BUCKETS - 1, h]

    def scores(kblk):
        return lax.dot_general(qs, kblk, _NT, preferred_element_type=F32)

    def both(b):
        return jnp.concatenate([b, b], axis=0)

    def update(carry, s, v):
        m, l, acc = carry
        m_new = jnp.maximum(m, jnp.max(s, axis=-1, keepdims=True))
        alpha = jnp.exp(m - m_new)
        p = jnp.exp(s - m_new)
        l = alpha * l + jnp.sum(p, axis=-1, keepdims=True)
        acc = alpha * acc + jnp.dot(p.astype(BF16), v, preferred_element_type=F32)
        return m_new, l, acc

    s = scores(km_ref[...]) + both(jnp.where(i == 0, bm_ref[...], far_bias))
    m = jnp.max(s, axis=-1, keepdims=True)
    p = jnp.exp(s - m)
    l = jnp.sum(p, axis=-1, keepdims=True)
    acc = jnp.dot(p.astype(BF16), vm_ref[...], preferred_element_type=F32)
    carry = (m, l, acc)

    def far_body(j, carry):
        off = pl.multiple_of(j * tq, tq)
        s = scores(k_ref[pl.ds(off, tq), :]) + far_bias
        return update(carry, s, v_ref[pl.ds(off, tq), :])

    carry = lax.fori_loop(0, jnp.maximum(i - 1, 0), far_body, carry)

    def sub_diag(carry):
        off = pl.multiple_of((i - 1) * tq, tq)
        s = scores(k_ref[pl.ds(off, tq), :]) + both(bd_ref[1])
        return update(carry, s, v_ref[pl.ds(off, tq), :])

    carry = lax.cond(i >= 1, sub_diag, lambda c: c, carry)

    off = pl.multiple_of(i * tq, tq)
    s = scores(k_ref[pl.ds(off, tq), :]) + both(bd_ref[0])
    m, l, acc = update(carry, s, v_ref[pl.ds(off, tq), :])

    o = acc / l
    o = o[:tq] - lam_ref[0, 0] * o[tq:]
    o = _rms(o, g_ref[...]) * (1.0 - lambda_init)
    o_ref[...] = o.astype(BF16)


def _attention(qkv, qkv_meta, lam, rel_bias, bias_d, bias_m, subln_g, *, tq, lambda_init):
    b, s, d3 = qkv.shape
    d = d3 // 3
    n_heads = d // V_DIM
    kern = functools.partial(_attn_kernel, lambda_init=lambda_init)
    return pl.pallas_call(
        kern,
        out_shape=jax.ShapeDtypeStruct((b, s, d), BF16),
        grid=(b, n_heads, s // tq),
        in_specs=[
            pl.BlockSpec(memory_space=pltpu.SMEM),
            pl.BlockSpec(memory_space=pltpu.SMEM),
            pl.BlockSpec((None, tq, V_DIM), lambda b, h, i: (b, i, h)),
            pl.BlockSpec((None, s, V_DIM), lambda b, h, i: (b, 0, n_heads + h)),
            pl.BlockSpec((None, s, V_DIM), lambda b, h, i: (b, 0, 2 * n_heads + h)),
            pl.BlockSpec((N_META, V_DIM), lambda b, h, i: (0, n_heads + h)),
            pl.BlockSpec((N_META, V_DIM), lambda b, h, i: (0, 2 * n_heads + h)),
            pl.BlockSpec((None, 2, tq, tq), lambda b, h, i: (h, 0, 0, 0)),
            pl.BlockSpec((None, tq, N_META), lambda b, h, i: (h, 0, 0)),
            pl.BlockSpec((1, V_DIM), lambda b, h, i: (0, 0)),
        ],
        out_specs=pl.BlockSpec((None, tq, V_DIM), lambda b, h, i: (b, i, h)),
        compiler_params=pltpu.CompilerParams(
            dimension_semantics=("arbitrary", "arbitrary", "arbitrary"),
            vmem_limit_bytes=V7X_VMEM_LIMIT),
        name="diff_attention",
    )(lam, rel_bias, qkv, qkv, qkv, qkv_meta, qkv_meta, bias_d, bias_m, subln_g)


def _layer1_kernel(a_ref, h_ref, wo_ref, g2_ref, wg_ref, wu_ref, wd_ref, gf_ref, o_ref, *, fc):
    hmid = h_ref[...] + jnp.dot(a_ref[...], wo_ref[...], preferred_element_type=F32)
    hn2 = _rms(hmid, g2_ref[...]).astype(BF16)
    h2 = hmid + _swiglu(hn2, wg_ref, wu_ref, wd_ref, fc)
    o_ref[...] = _rms(h2, gf_ref[...])


def _layer1(a, h, wo, g2, wg, wu, wd, gf, *, rows, fc):
    n, d = h.shape
    f = wg.shape[1]
    kern = functools.partial(_layer1_kernel, fc=fc)
    return pl.pallas_call(
        kern,
        out_shape=jax.ShapeDtypeStruct((n, d), F32),
        grid=(n // rows,),
        in_specs=[
            pl.BlockSpec((rows, d), lambda i: (i, 0)),
            pl.BlockSpec((rows, d), lambda i: (i, 0)),
            _resident((d, d)),
            _resident((1, d)),
            _resident((d, f)),
            _resident((d, f)),
            _resident((f, d)),
            _resident((1, d)),
        ],
        out_specs=pl.BlockSpec((rows, d), lambda i: (i, 0)),
        compiler_params=pltpu.CompilerParams(
            dimension_semantics=("arbitrary",), vmem_limit_bytes=V7X_VMEM_LIMIT),
        name="layer1_out_swiglu_norm",
    )(a, h, wo, g2, wg, wu, wd, gf)


def _pick_rows(seq, want):
    rows = min(seq, want)
    assert seq % rows == 0 and rows % HALO == 0
    return rows


def kernel(x, meta_tokens, rel_bias, mix_norm_g, ffn_norm_g, pool_w, pool_scale, attn_w_qkv, attn_w_o,
           lambda_q1, lambda_k1, lambda_q2, lambda_k2, subln_g, ffn_w_gate, ffn_w_up, ffn_w_down,
           final_norm_g):
    bsz, seq, d = x.shape
    assert meta_tokens.shape == (N_META, d) and N_META == HALO
    rows = _pick_rows(seq, 512)
    tq = _pick_rows(seq, 256)
    assert tq >= 113
    fc = 256
    lambda_init = 0.8 - 0.6 * math.exp(-0.3 * 1)

    row = lambda v: v.reshape(1, -1)
    wg, wu, wd = (w.astype(BF16) for w in (ffn_w_gate, ffn_w_up, ffn_w_down))
    pw = pool_w[0].astype(BF16)
    wqkv = attn_w_qkv[0].astype(BF16)
    wo = attn_w_o[0].astype(BF16)
    xr = x.reshape(bsz * seq, d)

    l0 = functools.partial(_layer0, g1=row(mix_norm_g[0]), pw=pw, ps=row(pool_scale[0]),
                           g2=row(ffn_norm_g[0]), wg=wg[0], wu=wu[0], wd=wd[0], fc=fc)
    h1 = l0(xr, meta_tokens, rows=rows, tiles_per_seq=seq // rows, pos0=N_META)
    h1_meta = l0(meta_tokens, jnp.zeros_like(meta_tokens), rows=N_META, tiles_per_seq=1, pos0=0)

    qkv = _qkv(h1, row(mix_norm_g[1]), wqkv, rows=rows)
    qkv_meta = _qkv(h1_meta, row(mix_norm_g[1]), wqkv, rows=N_META)

    lam_vecs = jnp.stack([lambda_q1[0], lambda_k1[0], lambda_q2[0], lambda_k2[0]])
    bias_d, bias_m, lam = _bias_tables(rel_bias, lam_vecs, tq=tq, lambda_init=lambda_init)
    attn = _attention(qkv.reshape(bsz, seq, 3 * d), qkv_meta, lam, rel_bias, bias_d, bias_m,
                      row(subln_g[0]), tq=tq, lambda_init=lambda_init)

    out = _layer1(attn.reshape(bsz * seq, d), h1, wo, row(ffn_norm_g[1]), wg[1], wu[1], wd[1],
                  row(final_norm_g), rows=rows, fc=fc)
    return out.reshape(bsz, seq, d)
```

```python
import functools
import math

import numpy as np
import jax
import jax.numpy as jnp
from jax import lax
from jax.experimental import pallas as pl
from jax.experimental.pallas import tpu as pltpu

N_META = 16
POOL_WINDOWS = (2, 4, 8, 16)
HEAD_DIM = 64
V_DIM = 2 * HEAD_DIM
N_BUCKETS = 32
MAX_DISTANCE = 128
RMS_EPS = 1e-6
NEG_INF = -1e30
LOG2E = math.log2(math.e)
HALO = 16
V7X_VMEM_LIMIT = 56 * 1024 * 1024

F32 = jnp.float32
BF16 = jnp.bfloat16
_NT = (((1,), (1,)), ((), ()))


def _rms(x, g):
    ms = jnp.mean(x * x, axis=-1, keepdims=True)
    return x * lax.rsqrt(ms + RMS_EPS) * g


def _swiglu(hn, wg_ref, wu_ref, wd_ref, fc):
    acc = None
    for c in range(wg_ref.shape[1] // fc):
        sl = slice(c * fc, (c + 1) * fc)
        g = jnp.dot(hn, wg_ref[:, sl], preferred_element_type=F32)
        u = jnp.dot(hn, wu_ref[:, sl], preferred_element_type=F32)
        a = (g * jax.nn.sigmoid(g) * u).astype(BF16)
        d = jnp.dot(a, wd_ref[sl, :], preferred_element_type=F32)
        acc = d if acc is None else acc + d
    return acc


def _layer0_kernel(x_ref, xprev_ref, first_ref, g1_ref, pw_ref, ps_ref, g2_ref,
                   wg_ref, wu_ref, wd_ref, o_ref, *, tiles_per_seq, pos0, fc):
    i = pl.program_id(0)
    rows = x_ref.shape[0]
    gc = pw_ref.shape[1]
    tile_in_seq = i % tiles_per_seq
    x = x_ref[...]
    g1 = g1_ref[...]
    halo_raw = jnp.where(tile_in_seq == 0, first_ref[...], xprev_ref[...])
    hn = _rms(x, g1)
    full = jnp.concatenate([_rms(halo_raw, g1), hn], axis=0)
    pos = lax.broadcasted_iota(jnp.int32, (rows, 1), 0) + tile_in_seq * rows + pos0
    mixes = []
    for g, w in enumerate(POOL_WINDOWS):
        cols = slice(g * gc, (g + 1) * gc)
        s = full[:, cols]
        k = 1
        while k < w:
            s = s + pltpu.roll(s, k, 0)
            k *= 2
        inv_cnt = 1.0 / jnp.minimum(pos + 1, w).astype(F32)
        pooled = s[HALO:] * inv_cnt - hn[:, cols]
        mixes.append(jnp.dot(pooled.astype(BF16), pw_ref[g], preferred_element_type=F32))
    hmid = x + jnp.concatenate(mixes, axis=-1) * ps_ref[...]
    hn2 = _rms(hmid, g2_ref[...]).astype(BF16)
    o_ref[...] = hmid + _swiglu(hn2, wg_ref, wu_ref, wd_ref, fc)


def _resident(shape):
    return pl.BlockSpec(shape, lambda *_: (0,) * len(shape), pipeline_mode=pl.Buffered(1))


def _layer0(x, first, g1, pw, ps, g2, wg, wu, wd, *, rows, tiles_per_seq, pos0, fc):
    n, d = x.shape
    f = wg.shape[1]
    halo_blocks = rows // HALO
    kern = functools.partial(_layer0_kernel, tiles_per_seq=tiles_per_seq, pos0=pos0, fc=fc)
    return pl.pallas_call(
        kern,
        out_shape=jax.ShapeDtypeStruct((n, d), F32),
        grid=(n // rows,),
        in_specs=[
            pl.BlockSpec((rows, d), lambda i: (i, 0)),
            pl.BlockSpec((HALO, d), lambda i: (jnp.maximum(i * halo_blocks - 1, 0), 0)),
            _resident((HALO, d)),
            _resident((1, d)),
            _resident(pw.shape),
            _resident((1, d)),
            _resident((1, d)),
            _resident((d, f)),
            _resident((d, f)),
            _resident((f, d)),
        ],
        out_specs=pl.BlockSpec((rows, d), lambda i: (i, 0)),
        compiler_params=pltpu.CompilerParams(
            dimension_semantics=("arbitrary",), vmem_limit_bytes=V7X_VMEM_LIMIT),
        name="layer0_pool_swiglu",
    )(x, x, first, g1, pw, ps, g2, wg, wu, wd)


def _qkv_kernel(h_ref, g_ref, w_ref, o_ref):
    d = h_ref.shape[1]
    hn = _rms(h_ref[...], g_ref[...]).astype(BF16)
    for c in range(3):
        sl = slice(c * d, (c + 1) * d)
        y = jnp.dot(hn, w_ref[:, sl], preferred_element_type=F32)
        if c == 0:
            y = y * (LOG2E / math.sqrt(HEAD_DIM))
        o_ref[:, sl] = y.astype(BF16)


def _qkv(h, g, w, *, rows):
    n, d = h.shape
    return pl.pallas_call(
        _qkv_kernel,
        out_shape=jax.ShapeDtypeStruct((n, 3 * d), BF16),
        grid=(n // rows,),
        in_specs=[
            pl.BlockSpec((rows, d), lambda i: (i, 0)),
            _resident((1, d)),
            _resident((d, 3 * d)),
        ],
        out_specs=pl.BlockSpec((rows, 3 * d), lambda i: (i, 0)),
        compiler_params=pltpu.CompilerParams(
            dimension_semantics=("arbitrary",), vmem_limit_bytes=V7X_VMEM_LIMIT),
        name="qkv_proj",
    )(h, g, w)


def _bucket_table(rel):
    n = np.maximum(rel, 0)
    max_exact = N_BUCKETS // 2
    nf = np.maximum(n, max_exact).astype(np.float32)
    large = max_exact + (np.log(nf / np.float32(max_exact)) / np.float32(math.log(MAX_DISTANCE / max_exact))
                         * np.float32(N_BUCKETS - max_exact)).astype(np.int32)
    large = np.minimum(large, N_BUCKETS - 1)
    return np.where(rel < 0, -1, np.where(n < max_exact, n, large)).astype(np.int32)


def _bias_kernel(rb_ref, lv_ref, idxd_ref, idxm_ref, bd_ref, bm_ref, lam_ref, *, lambda_init):
    h = pl.program_id(0)

    far = rb_ref[N_BUCKETS - 1, h]

    def build(idx):
        out = jnp.full(idx.shape, NEG_INF, F32)
        for b in range(N_BUCKETS):
            out = jnp.where(idx == b, (rb_ref[b, h] - far) * LOG2E, out)
        return out

    bd_ref[...] = build(idxd_ref[...])
    bm_ref[...] = build(idxm_ref[...])
    lv = lv_ref[...]
    e1 = jnp.exp(jnp.sum(lv[0:1] * lv[1:2], axis=-1, keepdims=True))
    e2 = jnp.exp(jnp.sum(lv[2:3] * lv[3:4], axis=-1, keepdims=True))
    lam_ref[...] = e1 - e2 + lambda_init


def _bias_tables(rel_bias, lam_vecs, *, tq, lambda_init):
    n_heads = rel_bias.shape[1]
    qi = np.arange(tq)[:, None]
    idx_d = np.stack([_bucket_table(d * tq + qi - np.arange(tq)[None, :]) for d in range(2)])
    idx_m = np.stack([_bucket_table(N_META + d * tq + qi - np.arange(N_META)[None, :]) for d in range(2)])
    kern = functools.partial(_bias_kernel, lambda_init=lambda_init)
    return pl.pallas_call(
        kern,
        out_shape=(jax.ShapeDtypeStruct((n_heads, 2, tq, tq), F32),
                   jax.ShapeDtypeStruct((n_heads, 2, tq, N_META), F32),
                   jax.ShapeDtypeStruct((1, 1), F32)),
        grid=(n_heads,),
        in_specs=[
            pl.BlockSpec(memory_space=pltpu.SMEM),
            pl.BlockSpec(lam_vecs.shape, lambda h: (0, 0)),
            pl.BlockSpec(idx_d.shape, lambda h: (0, 0, 0)),
            pl.BlockSpec(idx_m.shape, lambda h: (0, 0, 0)),
        ],
        out_specs=(pl.BlockSpec((None, 2, tq, tq), lambda h: (h, 0, 0, 0)),
                   pl.BlockSpec((None, 2, tq, N_META), lambda h: (h, 0, 0, 0)),
                   pl.BlockSpec((1, 1), lambda h: (0, 0))),
        compiler_params=pltpu.CompilerParams(dimension_semantics=("arbitrary",)),
        name="rel_bias_tiles",
    )(rel_bias, lam_vecs, jnp.asarray(idx_d), jnp.asarray(idx_m))


def _attn_kernel(lam_ref, q_ref, k_ref, v_ref, km_ref, vm_ref, bd_ref, bm_ref, g_ref,
                 o_ref, vext_ref, s_ref, p_ref, m_ref, acc_ref, *, lambda_init):
    i = pl.program_id(2)
    tq = q_ref.shape[0]
    ones = jnp.ones((1, V_DIM), BF16)

    @pl.when(i == 0)
    def _():
        vext_ref[:, :V_DIM] = v_ref[...]
        vext_ref[:, V_DIM:] = jnp.broadcast_to(ones, v_ref.shape)

    q = q_ref[...]
    lane = lax.broadcasted_iota(jnp.int32, q.shape, 1)
    zero = jnp.zeros_like(q)
    qs = jnp.concatenate([jnp.where(lane < HEAD_DIM, q, zero),
                          jnp.where(lane >= HEAD_DIM, q, zero)], axis=0)

    def both(b):
        return jnp.concatenate([b, b], axis=0)

    def scores(tile):
        off = pl.multiple_of(tile * tq, tq)
        return lax.dot_general(qs, k_ref[pl.ds(off, tq), :], _NT, preferred_element_type=F32)

    def weighted_values(p, tile):
        off = pl.multiple_of(tile * tq, tq)
        return jnp.dot(p, vext_ref[pl.ds(off, tq), :], preferred_element_type=F32)

    def row_max(s):
        return jnp.broadcast_to(jnp.max(s, axis=-1, keepdims=True), (s.shape[0], V_DIM))

    def wide(r, like):
        return jnp.concatenate([r] * (like.shape[1] // V_DIM), axis=1)

    s = scores(i) + both(bd_ref[0])
    m = row_max(s)
    p_ref[...] = jnp.exp2(s - wide(m, s)).astype(BF16)
    m_ref[...] = m
    acc_ref[...] = jnp.zeros_like(acc_ref)
    s_ref[...] = scores(jnp.maximum(i - 1, 0)) + both(bd_ref[1])

    @pl.loop(1, i + 1)
    def _(t):
        tile = i - t
        s_next = scores(jnp.maximum(tile - 1, 0))
        pv = weighted_values(p_ref[...], tile + 1)
        s = s_ref[...]
        m_old = m_ref[...]
        m_new = jnp.maximum(m_old, row_max(s))
        p_ref[...] = jnp.exp2(s - wide(m_new, s)).astype(BF16)
        m_ref[...] = m_new
        s_ref[...] = s_next
        acc_ref[...] = (acc_ref[...] + pv) * wide(jnp.exp2(m_old - m_new), pv)

    acc = acc_ref[...] + weighted_values(p_ref[...], 0)

    s = lax.dot_general(qs, km_ref[...], _NT, preferred_element_type=F32) + both(bm_ref[jnp.minimum(i, 1)])
    m = m_ref[...]
    m_new = jnp.maximum(m, row_max(s))
    p = jnp.exp2(s - m_new[:, :N_META]).astype(BF16)
    vm_ext = jnp.concatenate([vm_ref[...], jnp.broadcast_to(ones, vm_ref.shape)], axis=-1)
    acc = acc * wide(jnp.exp2(m - m_new), acc) + jnp.dot(p, vm_ext, preferred_element_type=F32)

    o = acc[:, :V_DIM] / acc[:, V_DIM:]
    o = o[:tq] - lam_ref[0, 0] * o[tq:]
    o = _rms(o, g_ref[...]) * (1.0 - lambda_init)
    o_ref[...] = o.astype(BF16)


def _attention(qkv, qkv_meta, lam, bias_d, bias_m, subln_g, *, tq, lambda_init):
    b, s, d3 = qkv.shape
    d = d3 // 3
    n_heads = d // V_DIM
    kern = functools.partial(_attn_kernel, lambda_init=lambda_init)
    return pl.pallas_call(
        kern,
        out_shape=jax.ShapeDtypeStruct((b, s, d), BF16),
        grid=(b, n_heads, s // tq),
        in_specs=[
            pl.BlockSpec(memory_space=pltpu.SMEM),
            pl.BlockSpec((None, tq, V_DIM), lambda b, h, i: (b, i, h)),
            pl.BlockSpec((None, s, V_DIM), lambda b, h, i: (b, 0, n_heads + h)),
            pl.BlockSpec((None, s, V_DIM), lambda b, h, i: (b, 0, 2 * n_heads + h)),
            pl.BlockSpec((N_META, V_DIM), lambda b, h, i: (0, n_heads + h)),
            pl.BlockSpec((N_META, V_DIM), lambda b, h, i: (0, 2 * n_heads + h)),
            pl.BlockSpec((None, 2, tq, tq), lambda b, h, i: (h, 0, 0, 0)),
            pl.BlockSpec((None, 2, tq, N_META), lambda b, h, i: (h, 0, 0, 0)),
            pl.BlockSpec((1, V_DIM), lambda b, h, i: (0, 0)),
        ],
        out_specs=pl.BlockSpec((None, tq, V_DIM), lambda b, h, i: (b, i, h)),
        scratch_shapes=[
            pltpu.VMEM((s, 2 * V_DIM), BF16),
            pltpu.VMEM((2 * tq, tq), F32),
            pltpu.VMEM((2 * tq, tq), BF16),
            pltpu.VMEM((2 * tq, V_DIM), F32),
            pltpu.VMEM((2 * tq, 2 * V_DIM), F32),
        ],
        compiler_params=pltpu.CompilerParams(
            dimension_semantics=("arbitrary", "arbitrary", "arbitrary"),
            vmem_limit_bytes=V7X_VMEM_LIMIT),
        name="diff_attention",
    )(lam, qkv, qkv, qkv, qkv_meta, qkv_meta, bias_d, bias_m, subln_g)


def _layer1_kernel(a_ref, h_ref, wo_ref, g2_ref, wg_ref, wu_ref, wd_ref, gf_ref, o_ref, *, fc):
    hmid = h_ref[...] + jnp.dot(a_ref[...], wo_ref[...], preferred_element_type=F32)
    hn2 = _rms(hmid, g2_ref[...]).astype(BF16)
    h2 = hmid + _swiglu(hn2, wg_ref, wu_ref, wd_ref, fc)
    o_ref[...] = _rms(h2, gf_ref[...])


def _layer1(a, h, wo, g2, wg, wu, wd, gf, *, rows, fc):
    n, d = h.shape
    f = wg.shape[1]
    kern = functools.partial(_layer1_kernel, fc=fc)
    return pl.pallas_call(
        kern,
        out_shape=jax.ShapeDtypeStruct((n, d), F32),
        grid=(n // rows,),
        in_specs=[
            pl.BlockSpec((rows, d), lambda i: (i, 0)),
            pl.BlockSpec((rows, d), lambda i: (i, 0)),
            _resident((d, d)),
            _resident((1, d)),
            _resident((d, f)),
            _resident((d, f)),
            _resident((f, d)),
            _resident((1, d)),
        ],
        out_specs=pl.BlockSpec((rows, d), lambda i: (i, 0)),
        compiler_params=pltpu.CompilerParams(
            dimension_semantics=("arbitrary",), vmem_limit_bytes=V7X_VMEM_LIMIT),
        name="layer1_out_swiglu_norm",
    )(a, h, wo, g2, wg, wu, wd, gf)


def _pick_rows(seq, want):
    rows = min(seq, want)
    assert seq % rows == 0 and rows % HALO == 0
    return rows


def kernel(x, meta_tokens, rel_bias, mix_norm_g, ffn_norm_g, pool_w, pool_scale, attn_w_qkv, attn_w_o,
           lambda_q1, lambda_k1, lambda_q2, lambda_k2, subln_g, ffn_w_gate, ffn_w_up, ffn_w_down,
           final_norm_g):
    bsz, seq, d = x.shape
    assert meta_tokens.shape == (N_META, d) and N_META == HALO
    rows = _pick_rows(seq, 512)
    tq = _pick_rows(seq, 256)
    assert tq >= 113
    fc = 256
    lambda_init = 0.8 - 0.6 * math.exp(-0.3 * 1)

    row = lambda v: v.reshape(1, -1)
    wg, wu, wd = (w.astype(BF16) for w in (ffn_w_gate, ffn_w_up, ffn_w_down))
    pw = pool_w[0].astype(BF16)
    wqkv = attn_w_qkv[0].astype(BF16)
    wo = attn_w_o[0].astype(BF16)
    xr = x.reshape(bsz * seq, d)

    l0 = functools.partial(_layer0, g1=row(mix_norm_g[0]), pw=pw, ps=row(pool_scale[0]),
                           g2=row(ffn_norm_g[0]), wg=wg[0], wu=wu[0], wd=wd[0], fc=fc)
    h1 = l0(xr, meta_tokens, rows=rows, tiles_per_seq=seq // rows, pos0=N_META)
    h1_meta = l0(meta_tokens, jnp.zeros_like(meta_tokens), rows=N_META, tiles_per_seq=1, pos0=0)

    qkv = _qkv(h1, row(mix_norm_g[1]), wqkv, rows=rows)
    qkv_meta = _qkv(h1_meta, row(mix_norm_g[1]), wqkv, rows=N_META)

    lam_vecs = jnp.stack([lambda_q1[0], lambda_k1[0], lambda_q2[0], lambda_k2[0]])
    bias_d, bias_m, lam = _bias_tables(rel_bias, lam_vecs, tq=tq, lambda_init=lambda_init)
    attn = _attention(qkv.reshape(bsz, seq, 3 * d), qkv_meta, lam, bias_d, bias_m,
                      row(subln_g[0]), tq=tq, lambda_init=lambda_init)

    out = _layer1(attn.reshape(bsz * seq, d), h1, wo, row(ffn_norm_g[1]), wg[1], wu[1], wd[1],
                  row(final_norm_g), rows=rows, fc=fc)
    return out.reshape(bsz, seq, d)
```

```python
import functools
import math

import numpy as np
import jax
import jax.numpy as jnp
from jax import lax
from jax.experimental import pallas as pl
from jax.experimental.pallas import tpu as pltpu

N_META = 16
POOL_WINDOWS = (2, 4, 8, 16)
HEAD_DIM = 64
V_DIM = 2 * HEAD_DIM
N_BUCKETS = 32
MAX_DISTANCE = 128
RMS_EPS = 1e-6
NEG_INF = -1e30
LOG2E = math.log2(math.e)
HALO = 16
V7X_VMEM_LIMIT = 56 * 1024 * 1024

F32 = jnp.float32
BF16 = jnp.bfloat16
_NT = (((1,), (1,)), ((), ()))


def _rms(x, g):
    ms = jnp.mean(x * x, axis=-1, keepdims=True)
    return x * lax.rsqrt(ms + RMS_EPS) * g


def _swiglu(hn, wg_ref, wu_ref, wd_ref, fc):
    acc = None
    for c in range(wg_ref.shape[1] // fc):
        sl = slice(c * fc, (c + 1) * fc)
        g = jnp.dot(hn, wg_ref[:, sl], preferred_element_type=F32)
        u = jnp.dot(hn, wu_ref[:, sl], preferred_element_type=F32)
        a = (g * jax.nn.sigmoid(g) * u).astype(BF16)
        d = jnp.dot(a, wd_ref[sl, :], preferred_element_type=F32)
        acc = d if acc is None else acc + d
    return acc


def _layer0_kernel(x_ref, xprev_ref, first_ref, g1_ref, pw_ref, ps_ref, g2_ref,
                   wg_ref, wu_ref, wd_ref, o_ref, *, tiles_per_seq, pos0, fc):
    i = pl.program_id(0)
    rows = x_ref.shape[0]
    gc = pw_ref.shape[1]
    tile_in_seq = i % tiles_per_seq
    x = x_ref[...]
    g1 = g1_ref[...]
    halo_raw = jnp.where(tile_in_seq == 0, first_ref[...], xprev_ref[...])
    hn = _rms(x, g1)
    full = jnp.concatenate([_rms(halo_raw, g1), hn], axis=0)
    pos = lax.broadcasted_iota(jnp.int32, (rows, 1), 0) + tile_in_seq * rows + pos0
    mixes = []
    for g, w in enumerate(POOL_WINDOWS):
        cols = slice(g * gc, (g + 1) * gc)
        s = full[:, cols]
        k = 1
        while k < w:
            s = s + pltpu.roll(s, k, 0)
            k *= 2
        inv_cnt = 1.0 / jnp.minimum(pos + 1, w).astype(F32)
        pooled = s[HALO:] * inv_cnt - hn[:, cols]
        mixes.append(jnp.dot(pooled.astype(BF16), pw_ref[g], preferred_element_type=F32))
    hmid = x + jnp.concatenate(mixes, axis=-1) * ps_ref[...]
    hn2 = _rms(hmid, g2_ref[...]).astype(BF16)
    o_ref[...] = hmid + _swiglu(hn2, wg_ref, wu_ref, wd_ref, fc)


def _resident(shape):
    return pl.BlockSpec(shape, lambda *_: (0,) * len(shape), pipeline_mode=pl.Buffered(1))


def _layer0(x, first, g1, pw, ps, g2, wg, wu, wd, *, rows, tiles_per_seq, pos0, fc):
    n, d = x.shape
    f = wg.shape[1]
    halo_blocks = rows // HALO
    kern = functools.partial(_layer0_kernel, tiles_per_seq=tiles_per_seq, pos0=pos0, fc=fc)
    return pl.pallas_call(
        kern,
        out_shape=jax.ShapeDtypeStruct((n, d), F32),
        grid=(n // rows,),
        in_specs=[
            pl.BlockSpec((rows, d), lambda i: (i, 0)),
            pl.BlockSpec((HALO, d), lambda i: (jnp.maximum(i * halo_blocks - 1, 0), 0)),
            _resident((HALO, d)),
            _resident((1, d)),
            _resident(pw.shape),
            _resident((1, d)),
            _resident((1, d)),
            _resident((d, f)),
            _resident((d, f)),
            _resident((f, d)),
        ],
        out_specs=pl.BlockSpec((rows, d), lambda i: (i, 0)),
        compiler_params=pltpu.CompilerParams(
            dimension_semantics=("arbitrary",), vmem_limit_bytes=V7X_VMEM_LIMIT),
        name="layer0_pool_swiglu",
    )(x, x, first, g1, pw, ps, g2, wg, wu, wd)


def _qkv_kernel(h_ref, g_ref, w_ref, o_ref):
    d = h_ref.shape[1]
    hn = _rms(h_ref[...], g_ref[...]).astype(BF16)
    for c in range(3):
        sl = slice(c * d, (c + 1) * d)
        y = jnp.dot(hn, w_ref[:, sl], preferred_element_type=F32)
        if c == 0:
            y = y * (LOG2E / math.sqrt(HEAD_DIM))
        o_ref[:, sl] = y.astype(BF16)


def _qkv(h, g, w, *, rows):
    n, d = h.shape
    return pl.pallas_call(
        _qkv_kernel,
        out_shape=jax.ShapeDtypeStruct((n, 3 * d), BF16),
        grid=(n // rows,),
        in_specs=[
            pl.BlockSpec((rows, d), lambda i: (i, 0)),
            _resident((1, d)),
            _resident((d, 3 * d)),
        ],
        out_specs=pl.BlockSpec((rows, 3 * d), lambda i: (i, 0)),
        compiler_params=pltpu.CompilerParams(
            dimension_semantics=("arbitrary",), vmem_limit_bytes=V7X_VMEM_LIMIT),
        name="qkv_proj",
    )(h, g, w)


def _bucket_table(rel):
    n = np.maximum(rel, 0)
    max_exact = N_BUCKETS // 2
    nf = np.maximum(n, max_exact).astype(np.float32)
    large = max_exact + (np.log(nf / np.float32(max_exact)) / np.float32(math.log(MAX_DISTANCE / max_exact))
                         * np.float32(N_BUCKETS - max_exact)).astype(np.int32)
    large = np.minimum(large, N_BUCKETS - 1)
    return np.where(rel < 0, -1, np.where(n < max_exact, n, large)).astype(np.int32)


def _bias_kernel(rb_ref, lv_ref, idxd_ref, idxm_ref, bd_ref, bm_ref, lam_ref, *, lambda_init):
    h = pl.program_id(0)
    far = rb_ref[N_BUCKETS - 1, h]

    def build(idx):
        out = jnp.full(idx.shape, NEG_INF, F32)
        for b in range(N_BUCKETS):
            out = jnp.where(idx == b, (rb_ref[b, h] - far) * LOG2E, out)
        return out

    bd_ref[...] = build(idxd_ref[...])
    bm_ref[...] = build(idxm_ref[...])
    lv = lv_ref[...]
    e1 = jnp.exp(jnp.sum(lv[0:1] * lv[1:2], axis=-1, keepdims=True))
    e2 = jnp.exp(jnp.sum(lv[2:3] * lv[3:4], axis=-1, keepdims=True))
    lam_ref[...] = e1 - e2 + lambda_init


def _bias_tables(rel_bias, lam_vecs, *, tq, lambda_init):
    n_heads = rel_bias.shape[1]
    qi = np.arange(tq)[:, None]
    idx_d = np.stack([_bucket_table(d * tq + qi - np.arange(tq)[None, :]) for d in range(3)])
    idx_m = np.stack([_bucket_table(N_META + d * tq + qi - np.arange(N_META)[None, :]) for d in range(2)])
    kern = functools.partial(_bias_kernel, lambda_init=lambda_init)
    return pl.pallas_call(
        kern,
        out_shape=(jax.ShapeDtypeStruct((n_heads, 3, tq, tq), F32),
                   jax.ShapeDtypeStruct((n_heads, 2, tq, N_META), F32),
                   jax.ShapeDtypeStruct((1, 1), F32)),
        grid=(n_heads,),
        in_specs=[
            pl.BlockSpec(memory_space=pltpu.SMEM),
            pl.BlockSpec(lam_vecs.shape, lambda h: (0, 0)),
            pl.BlockSpec(idx_d.shape, lambda h: (0, 0, 0)),
            pl.BlockSpec(idx_m.shape, lambda h: (0, 0, 0)),
        ],
        out_specs=(pl.BlockSpec((None, 3, tq, tq), lambda h: (h, 0, 0, 0)),
                   pl.BlockSpec((None, 2, tq, N_META), lambda h: (h, 0, 0, 0)),
                   pl.BlockSpec((1, 1), lambda h: (0, 0))),
        compiler_params=pltpu.CompilerParams(dimension_semantics=("arbitrary",)),
        name="rel_bias_tiles",
    )(rel_bias, lam_vecs, jnp.asarray(idx_d), jnp.asarray(idx_m))


def _attn_kernel(lam_ref, q_ref, k_ref, v_ref, km_ref, vm_ref, bd_ref, bm_ref, g_ref,
                 o_ref, vext_ref, s_ref, p_ref, m_ref, acc_ref, *, lambda_init, heads):
    i = pl.program_id(2)
    tq = q_ref.shape[0]
    ones = jnp.ones((1, V_DIM), BF16)
    lanes = [slice(u * V_DIM, (u + 1) * V_DIM) for u in range(heads)]

    @pl.when(i == 0)
    def _():
        for u in range(heads):
            vext_ref[u, :tq, :] = jnp.zeros((tq, 2 * V_DIM), BF16)
            vext_ref[u, :N_META, :V_DIM] = vm_ref[:, lanes[u]]
            vext_ref[u, :N_META, V_DIM:] = jnp.broadcast_to(ones, (N_META, V_DIM))
            vext_ref[u, tq:, :V_DIM] = v_ref[:, lanes[u]]
            vext_ref[u, tq:, V_DIM:] = jnp.broadcast_to(ones, (v_ref.shape[0], V_DIM))

    lane = lax.broadcasted_iota(jnp.int32, (tq, V_DIM), 1)

    def stacked_q(u):
        q = q_ref[:, lanes[u]]
        zero = jnp.zeros_like(q)
        return jnp.concatenate([jnp.where(lane < HEAD_DIM, q, zero),
                                jnp.where(lane >= HEAD_DIM, q, zero)], axis=0)

    qs = [stacked_q(u) for u in range(heads)]

    def both(b):
        return jnp.concatenate([b, b], axis=0)

    def scores(u, tile, dist):
        off = pl.multiple_of(tile * tq, tq)
        s = lax.dot_general(qs[u], k_ref[pl.ds(off, tq), lanes[u]], _NT, preferred_element_type=F32)
        return s + both(bd_ref[u, dist])

    def weighted_values(u, block):
        off = pl.multiple_of(block * tq, tq)
        return jnp.dot(p_ref[u], vext_ref[u, pl.ds(off, tq), :], preferred_element_type=F32)

    def row_max(s):
        return jnp.broadcast_to(jnp.max(s, axis=-1, keepdims=True), (s.shape[0], V_DIM))

    def wide(r, like):
        return jnp.concatenate([r] * (like.shape[1] // V_DIM), axis=1)

    for u in range(heads):
        s_ref[u] = scores(u, i, 0)
    for u in range(heads):
        s = lax.dot_general(qs[u], km_ref[:, lanes[u]], _NT, preferred_element_type=F32)
        s = s + both(bm_ref[u, jnp.minimum(i, 1)])
        m = row_max(s)
        m_ref[u] = m
        p_ref[u] = jnp.zeros(p_ref.shape[1:], BF16)
        p_ref[u, :, :N_META] = jnp.exp2(s - m[:, :N_META]).astype(BF16)
        acc_ref[u] = jnp.zeros(acc_ref.shape[1:], F32)

    @pl.loop(0, i + 1)
    def _(t):
        tile = i - t
        for u in range(heads):
            s_next = scores(u, jnp.maximum(tile - 1, 0), jnp.minimum(t + 1, 2))
            pv = weighted_values(u, jnp.where(t == 0, 0, tile + 2))
            s = s_ref[u]
            m_old = m_ref[u]
            m_new = jnp.maximum(m_old, row_max(s))
            p_ref[u] = jnp.exp2(s - wide(m_new, s)).astype(BF16)
            m_ref[u] = m_new
            s_ref[u] = s_next
            acc_ref[u] = (acc_ref[u] + pv) * wide(jnp.exp2(m_old - m_new), pv)

    for u in range(heads):
        acc = acc_ref[u] + weighted_values(u, 1)
        o = acc[:, :V_DIM] / acc[:, V_DIM:]
        o = o[:tq] - lam_ref[0, 0] * o[tq:]
        o = _rms(o, g_ref[...]) * (1.0 - lambda_init)
        o_ref[:, lanes[u]] = o.astype(BF16)


def _attention(qkv, qkv_meta, lam, bias_d, bias_m, subln_g, *, tq, lambda_init, heads):
    b, s, d3 = qkv.shape
    d = d3 // 3
    n_groups = d // (V_DIM * heads)
    w = V_DIM * heads
    kern = functools.partial(_attn_kernel, lambda_init=lambda_init, heads=heads)
    return pl.pallas_call(
        kern,
        out_shape=jax.ShapeDtypeStruct((b, s, d), BF16),
        grid=(b, n_groups, s // tq),
        in_specs=[
            pl.BlockSpec(memory_space=pltpu.SMEM),
            pl.BlockSpec((None, tq, w), lambda b, g, i: (b, i, g)),
            pl.BlockSpec((None, s, w), lambda b, g, i: (b, 0, n_groups + g)),
            pl.BlockSpec((None, s, w), lambda b, g, i: (b, 0, 2 * n_groups + g)),
            pl.BlockSpec((N_META, w), lambda b, g, i: (0, n_groups + g)),
            pl.BlockSpec((N_META, w), lambda b, g, i: (0, 2 * n_groups + g)),
            pl.BlockSpec((heads, 3, tq, tq), lambda b, g, i: (g, 0, 0, 0)),
            pl.BlockSpec((heads, 2, tq, N_META), lambda b, g, i: (g, 0, 0, 0)),
            pl.BlockSpec((1, V_DIM), lambda b, g, i: (0, 0)),
        ],
        out_specs=pl.BlockSpec((None, tq, w), lambda b, g, i: (b, i, g)),
        scratch_shapes=[
            pltpu.VMEM((heads, tq + s, 2 * V_DIM), BF16),
            pltpu.VMEM((heads, 2 * tq, tq), F32),
            pltpu.VMEM((heads, 2 * tq, tq), BF16),
            pltpu.VMEM((heads, 2 * tq, V_DIM), F32),
            pltpu.VMEM((heads, 2 * tq, 2 * V_DIM), F32),
        ],
        compiler_params=pltpu.CompilerParams(
            dimension_semantics=("arbitrary", "arbitrary", "arbitrary"),
            vmem_limit_bytes=V7X_VMEM_LIMIT),
        name="diff_attention",
    )(lam, qkv, qkv, qkv, qkv_meta, qkv_meta, bias_d, bias_m, subln_g)


def _layer1_kernel(a_ref, h_ref, wo_ref, g2_ref, wg_ref, wu_ref, wd_ref, gf_ref, o_ref, *, fc):
    hmid = h_ref[...] + jnp.dot(a_ref[...], wo_ref[...], preferred_element_type=F32)
    hn2 = _rms(hmid, g2_ref[...]).astype(BF16)
    h2 = hmid + _swiglu(hn2, wg_ref, wu_ref, wd_ref, fc)
    o_ref[...] = _rms(h2, gf_ref[...])


def _layer1(a, h, wo, g2, wg, wu, wd, gf, *, rows, fc):
    n, d = h.shape
    f = wg.shape[1]
    kern = functools.partial(_layer1_kernel, fc=fc)
    return pl.pallas_call(
        kern,
        out_shape=jax.ShapeDtypeStruct((n, d), F32),
        grid=(n // rows,),
        in_specs=[
            pl.BlockSpec((rows, d), lambda i: (i, 0)),
            pl.BlockSpec((rows, d), lambda i: (i, 0)),
            _resident((d, d)),
            _resident((1, d)),
            _resident((d, f)),
            _resident((d, f)),
            _resident((f, d)),
            _resident((1, d)),
        ],
        out_specs=pl.BlockSpec((rows, d), lambda i: (i, 0)),
        compiler_params=pltpu.CompilerParams(
            dimension_semantics=("arbitrary",), vmem_limit_bytes=V7X_VMEM_LIMIT),
        name="layer1_out_swiglu_norm",
    )(a, h, wo, g2, wg, wu, wd, gf)


ATTN_HEADS_PER_STEP = 2


def _pick_rows(seq, want):
    rows = min(seq, want)
    assert seq % rows == 0 and rows % HALO == 0
    return rows


def kernel(x, meta_tokens, rel_bias, mix_norm_g, ffn_norm_g, pool_w, pool_scale, attn_w_qkv, attn_w_o,
           lambda_q1, lambda_k1, lambda_q2, lambda_k2, subln_g, ffn_w_gate, ffn_w_up, ffn_w_down,
           final_norm_g):
    bsz, seq, d = x.shape
    assert meta_tokens.shape == (N_META, d) and N_META == HALO
    rows = _pick_rows(seq, 512)
    tq = _pick_rows(seq, 256)
    assert tq >= 113
    fc = 256
    lambda_init = 0.8 - 0.6 * math.exp(-0.3 * 1)

    row = lambda v: v.reshape(1, -1)
    wg, wu, wd = (w.astype(BF16) for w in (ffn_w_gate, ffn_w_up, ffn_w_down))
    pw = pool_w[0].astype(BF16)
    wqkv = attn_w_qkv[0].astype(BF16)
    wo = attn_w_o[0].astype(BF16)
    xr = x.reshape(bsz * seq, d)

    l0 = functools.partial(_layer0, g1=row(mix_norm_g[0]), pw=pw, ps=row(pool_scale[0]),
                           g2=row(ffn_norm_g[0]), wg=wg[0], wu=wu[0], wd=wd[0], fc=fc)
    h1 = l0(xr, meta_tokens, rows=rows, tiles_per_seq=seq // rows, pos0=N_META)
    h1_meta = l0(meta_tokens, jnp.zeros_like(meta_tokens), rows=N_META, tiles_per_seq=1, pos0=0)

    qkv = _qkv(h1, row(mix_norm_g[1]), wqkv, rows=rows)
    qkv_meta = _qkv(h1_meta, row(mix_norm_g[1]), wqkv, rows=N_META)

    lam_vecs = jnp.stack([lambda_q1[0], lambda_k1[0], lambda_q2[0], lambda_k2[0]])
    bias_d, bias_m, lam = _bias_tables(rel_bias, lam_vecs, tq=tq, lambda_init=lambda_init)
    attn = _attention(qkv.reshape(bsz, seq, 3 * d), qkv_meta, lam, bias_d, bias_m,
                      row(subln_g[0]), tq=tq, lambda_init=lambda_init, heads=ATTN_HEADS_PER_STEP)

    out = _layer1(attn.reshape(bsz * seq, d), h1, wo, row(ffn_norm_g[1]), wg[1], wu[1], wd[1],
                  row(final_norm_g), rows=rows, fc=fc)
    return out.reshape(bsz, seq, d)
```

```python
import functools
import math

import numpy as np
import jax
import jax.numpy as jnp
from jax import lax
from jax.experimental import pallas as pl
from jax.experimental.pallas import tpu as pltpu

N_META = 16
POOL_WINDOWS = (2, 4, 8, 16)
HEAD_DIM = 64
V_DIM = 2 * HEAD_DIM
N_BUCKETS = 32
MAX_DISTANCE = 128
RMS_EPS = 1e-6
NEG_INF = -1e30
LOG2E = math.log2(math.e)
HALO = 16
V7X_VMEM_LIMIT = 56 * 1024 * 1024

F32 = jnp.float32
BF16 = jnp.bfloat16
_NT = (((1,), (1,)), ((), ()))


def _rms(x, g):
    ms = jnp.mean(x * x, axis=-1, keepdims=True)
    return x * lax.rsqrt(ms + RMS_EPS) * g


def _swiglu(hn, wg_ref, wu_ref, wd_ref, fc):
    acc = None
    for c in range(wg_ref.shape[1] // fc):
        sl = slice(c * fc, (c + 1) * fc)
        g = jnp.dot(hn, wg_ref[:, sl], preferred_element_type=F32)
        u = jnp.dot(hn, wu_ref[:, sl], preferred_element_type=F32)
        a = (g * jax.nn.sigmoid(g) * u).astype(BF16)
        d = jnp.dot(a, wd_ref[sl, :], preferred_element_type=F32)
        acc = d if acc is None else acc + d
    return acc


def _layer0_kernel(x_ref, xprev_ref, first_ref, g1_ref, pw_ref, ps_ref, g2_ref,
                   wg_ref, wu_ref, wd_ref, o_ref, *, tiles_per_seq, pos0, fc):
    i = pl.program_id(0)
    rows = x_ref.shape[0]
    gc = pw_ref.shape[1]
    tile_in_seq = i % tiles_per_seq
    x = x_ref[...]
    g1 = g1_ref[...]
    halo_raw = jnp.where(tile_in_seq == 0, first_ref[...], xprev_ref[...])
    hn = _rms(x, g1)
    full = jnp.concatenate([_rms(halo_raw, g1), hn], axis=0)
    pos = lax.broadcasted_iota(jnp.int32, (rows, 1), 0) + tile_in_seq * rows + pos0
    mixes = []
    for g, w in enumerate(POOL_WINDOWS):
        cols = slice(g * gc, (g + 1) * gc)
        s = full[:, cols]
        k = 1
        while k < w:
            s = s + pltpu.roll(s, k, 0)
            k *= 2
        inv_cnt = 1.0 / jnp.minimum(pos + 1, w).astype(F32)
        pooled = s[HALO:] * inv_cnt - hn[:, cols]
        mixes.append(jnp.dot(pooled.astype(BF16), pw_ref[g], preferred_element_type=F32))
    hmid = x + jnp.concatenate(mixes, axis=-1) * ps_ref[...]
    hn2 = _rms(hmid, g2_ref[...]).astype(BF16)
    o_ref[...] = hmid + _swiglu(hn2, wg_ref, wu_ref, wd_ref, fc)


def _resident(shape):
    return pl.BlockSpec(shape, lambda *_: (0,) * len(shape), pipeline_mode=pl.Buffered(1))


def _layer0(x, first, g1, pw, ps, g2, wg, wu, wd, *, rows, tiles_per_seq, pos0, fc):
    n, d = x.shape
    f = wg.shape[1]
    halo_blocks = rows // HALO
    kern = functools.partial(_layer0_kernel, tiles_per_seq=tiles_per_seq, pos0=pos0, fc=fc)
    return pl.pallas_call(
        kern,
        out_shape=jax.ShapeDtypeStruct((n, d), F32),
        grid=(n // rows,),
        in_specs=[
            pl.BlockSpec((rows, d), lambda i: (i, 0)),
            pl.BlockSpec((HALO, d), lambda i: (jnp.maximum(i * halo_blocks - 1, 0), 0)),
            _resident((HALO, d)),
            _resident((1, d)),
            _resident(pw.shape),
            _resident((1, d)),
            _resident((1, d)),
            _resident((d, f)),
            _resident((d, f)),
            _resident((f, d)),
        ],
        out_specs=pl.BlockSpec((rows, d), lambda i: (i, 0)),
        compiler_params=pltpu.CompilerParams(
            dimension_semantics=("arbitrary",), vmem_limit_bytes=V7X_VMEM_LIMIT),
        name="layer0_pool_swiglu",
    )(x, x, first, g1, pw, ps, g2, wg, wu, wd)


def _qkv_kernel(h_ref, g_ref, w_ref, o_ref):
    d = h_ref.shape[1]
    hn = _rms(h_ref[...], g_ref[...]).astype(BF16)
    for c in range(3):
        sl = slice(c * d, (c + 1) * d)
        y = jnp.dot(hn, w_ref[:, sl], preferred_element_type=F32)
        if c == 0:
            y = y * (LOG2E / math.sqrt(HEAD_DIM))
        o_ref[:, sl] = y.astype(BF16)


def _qkv(h, g, w, *, rows):
    n, d = h.shape
    return pl.pallas_call(
        _qkv_kernel,
        out_shape=jax.ShapeDtypeStruct((n, 3 * d), BF16),
        grid=(n // rows,),
        in_specs=[
            pl.BlockSpec((rows, d), lambda i: (i, 0)),
            _resident((1, d)),
            _resident((d, 3 * d)),
        ],
        out_specs=pl.BlockSpec((rows, 3 * d), lambda i: (i, 0)),
        compiler_params=pltpu.CompilerParams(
            dimension_semantics=("arbitrary",), vmem_limit_bytes=V7X_VMEM_LIMIT),
        name="qkv_proj",
    )(h, g, w)


def _bucket_table(rel):
    n = np.maximum(rel, 0)
    max_exact = N_BUCKETS // 2
    nf = np.maximum(n, max_exact).astype(np.float32)
    large = max_exact + (np.log(nf / np.float32(max_exact)) / np.float32(math.log(MAX_DISTANCE / max_exact))
                         * np.float32(N_BUCKETS - max_exact)).astype(np.int32)
    large = np.minimum(large, N_BUCKETS - 1)
    return np.where(rel < 0, -1, np.where(n < max_exact, n, large)).astype(np.int32)


def _bias_kernel(rb_ref, lv_ref, idxd_ref, idxm_ref, bd_ref, bm_ref, lam_ref, *, lambda_init):
    h = pl.program_id(0)
    far = rb_ref[N_BUCKETS - 1, h]

    def build(idx):
        out = jnp.full(idx.shape, NEG_INF, F32)
        for b in range(N_BUCKETS):
            out = jnp.where(idx == b, (rb_ref[b, h] - far) * LOG2E, out)
        return out

    bd_ref[...] = build(idxd_ref[...])
    bm_ref[...] = build(idxm_ref[...])
    lv = lv_ref[...]
    e1 = jnp.exp(jnp.sum(lv[0:1] * lv[1:2], axis=-1, keepdims=True))
    e2 = jnp.exp(jnp.sum(lv[2:3] * lv[3:4], axis=-1, keepdims=True))
    lam_ref[...] = e1 - e2 + lambda_init


def _bias_tables(rel_bias, lam_vecs, *, tq, lambda_init):
    n_heads = rel_bias.shape[1]
    qi = np.arange(tq)[:, None]
    idx_d = np.stack([_bucket_table(d * tq + qi - np.arange(tq)[None, :]) for d in range(3)])
    idx_m = np.stack([_bucket_table(N_META + d * tq + qi - np.arange(N_META)[None, :]) for d in range(2)])
    kern = functools.partial(_bias_kernel, lambda_init=lambda_init)
    return pl.pallas_call(
        kern,
        out_shape=(jax.ShapeDtypeStruct((n_heads, 3, tq, tq), F32),
                   jax.ShapeDtypeStruct((n_heads, 2, tq, N_META), F32),
                   jax.ShapeDtypeStruct((1, 1), F32)),
        grid=(n_heads,),
        in_specs=[
            pl.BlockSpec(memory_space=pltpu.SMEM),
            pl.BlockSpec(lam_vecs.shape, lambda h: (0, 0)),
            pl.BlockSpec(idx_d.shape, lambda h: (0, 0, 0)),
            pl.BlockSpec(idx_m.shape, lambda h: (0, 0, 0)),
        ],
        out_specs=(pl.BlockSpec((None, 3, tq, tq), lambda h: (h, 0, 0, 0)),
                   pl.BlockSpec((None, 2, tq, N_META), lambda h: (h, 0, 0, 0)),
                   pl.BlockSpec((1, 1), lambda h: (0, 0))),
        compiler_params=pltpu.CompilerParams(dimension_semantics=("arbitrary",)),
        name="rel_bias_tiles",
    )(rel_bias, lam_vecs, jnp.asarray(idx_d), jnp.asarray(idx_m))


def _attn_kernel(lam_ref, q_ref, k_ref, v_ref, km_ref, vm_ref, bd_ref, bm_ref, g_ref,
                 o_ref, vext_ref, s_ref, p_ref, m_ref, acc_ref, *, lambda_init, heads):
    i = pl.program_id(2)
    tq = q_ref.shape[0]
    ones = jnp.ones((1, V_DIM), BF16)
    lanes = [slice(u * V_DIM, (u + 1) * V_DIM) for u in range(heads)]

    @pl.when(i == 0)
    def _():
        for u in range(heads):
            vext_ref[u, :tq, :] = jnp.zeros((tq, 2 * V_DIM), BF16)
            vext_ref[u, :N_META, :V_DIM] = vm_ref[:, lanes[u]]
            vext_ref[u, :N_META, V_DIM:] = jnp.broadcast_to(ones, (N_META, V_DIM))
            vext_ref[u, tq:, :V_DIM] = v_ref[:, lanes[u]]
            vext_ref[u, tq:, V_DIM:] = jnp.broadcast_to(ones, (v_ref.shape[0], V_DIM))

    lane = lax.broadcasted_iota(jnp.int32, (tq, V_DIM), 1)

    def stacked_q(u):
        q = q_ref[:, lanes[u]]
        zero = jnp.zeros_like(q)
        return jnp.concatenate([jnp.where(lane < HEAD_DIM, q, zero),
                                jnp.where(lane >= HEAD_DIM, q, zero)], axis=0)

    qs = [stacked_q(u) for u in range(heads)]

    def both(b):
        return jnp.concatenate([b, b], axis=0)

    def scores(u, tile, dist):
        off = pl.multiple_of(tile * tq, tq)
        s = lax.dot_general(qs[u], k_ref[pl.ds(off, tq), lanes[u]], _NT, preferred_element_type=F32)
        return s + both(bd_ref[u, dist])

    def weighted_values(u, block):
        off = pl.multiple_of(block * tq, tq)
        return jnp.dot(p_ref[u], vext_ref[u, pl.ds(off, tq), :], preferred_element_type=F32)

    def row_max(s):
        return jnp.broadcast_to(jnp.max(s, axis=-1, keepdims=True), (s.shape[0], V_DIM))

    def wide(r, like):
        return jnp.concatenate([r] * (like.shape[1] // V_DIM), axis=1)

    for u in range(heads):
        s_ref[u] = scores(u, i, 0)
    for u in range(heads):
        s = lax.dot_general(qs[u], km_ref[:, lanes[u]], _NT, preferred_element_type=F32)
        s = s + both(bm_ref[u, jnp.minimum(i, 1)])
        m = row_max(s)
        m_ref[u] = m
        p_ref[u] = jnp.zeros(p_ref.shape[1:], BF16)
        p_ref[u, :, :N_META] = jnp.exp2(s - m[:, :N_META]).astype(BF16)
        acc_ref[u] = jnp.zeros(acc_ref.shape[1:], F32)

    @pl.loop(0, i + 1)
    def _(t):
        tile = i - t
        for u in range(heads):
            s_next = scores(u, jnp.maximum(tile - 1, 0), jnp.minimum(t + 1, 2))
            pv = weighted_values(u, jnp.where(t == 0, 0, tile + 2))
            s = s_ref[u]
            m_old = m_ref[u]
            m_new = jnp.maximum(m_old, row_max(s))
            p_ref[u] = jnp.exp2(s - wide(m_new, s)).astype(BF16)
            m_ref[u] = m_new
            s_ref[u] = s_next
            acc_ref[u] = (acc_ref[u] + pv) * wide(jnp.exp2(m_old - m_new), pv)

    for u in range(heads):
        acc = acc_ref[u] + weighted_values(u, 1)
        o = acc[:, :V_DIM] / acc[:, V_DIM:]
        o = o[:tq] - lam_ref[0, 0] * o[tq:]
        o = _rms(o, g_ref[...]) * (1.0 - lambda_init)
        o_ref[:, lanes[u]] = o.astype(BF16)


def _attention(qkv, qkv_meta, lam, bias_d, bias_m, subln_g, *, tq, lambda_init, heads):
    b, s, d3 = qkv.shape
    d = d3 // 3
    n_groups = d // (V_DIM * heads)
    w = V_DIM * heads
    kern = functools.partial(_attn_kernel, lambda_init=lambda_init, heads=heads)
    return pl.pallas_call(
        kern,
        out_shape=jax.ShapeDtypeStruct((b, s, d), BF16),
        grid=(b, n_groups, s // tq),
        in_specs=[
            pl.BlockSpec(memory_space=pltpu.SMEM),
            pl.BlockSpec((None, tq, w), lambda b, g, i: (b, i, g)),
            pl.BlockSpec((None, s, w), lambda b, g, i: (b, 0, n_groups + g)),
            pl.BlockSpec((None, s, w), lambda b, g, i: (b, 0, 2 * n_groups + g)),
            pl.BlockSpec((N_META, w), lambda b, g, i: (0, n_groups + g)),
            pl.BlockSpec((N_META, w), lambda b, g, i: (0, 2 * n_groups + g)),
            pl.BlockSpec((heads, 3, tq, tq), lambda b, g, i: (g, 0, 0, 0)),
            pl.BlockSpec((heads, 2, tq, N_META), lambda b, g, i: (g, 0, 0, 0)),
            pl.BlockSpec((1, V_DIM), lambda b, g, i: (0, 0)),
        ],
        out_specs=pl.BlockSpec((None, tq, w), lambda b, g, i: (b, i, g)),
        scratch_shapes=[
            pltpu.VMEM((heads, tq + s, 2 * V_DIM), BF16),
            pltpu.VMEM((heads, 2 * tq, tq), F32),
            pltpu.VMEM((heads, 2 * tq, tq), BF16),
            pltpu.VMEM((heads, 2 * tq, V_DIM), F32),
            pltpu.VMEM((heads, 2 * tq, 2 * V_DIM), F32),
        ],
        compiler_params=pltpu.CompilerParams(
            dimension_semantics=("arbitrary", "arbitrary", "arbitrary"),
            vmem_limit_bytes=V7X_VMEM_LIMIT),
        name="diff_attention",
    )(lam, qkv, qkv, qkv, qkv_meta, qkv_meta, bias_d, bias_m, subln_g)


def _layer1_kernel(a_ref, h_ref, wo_ref, g2_ref, wg_ref, wu_ref, wd_ref, gf_ref, o_ref, *, fc):
    hmid = h_ref[...] + jnp.dot(a_ref[...], wo_ref[...], preferred_element_type=F32)
    hn2 = _rms(hmid, g2_ref[...]).astype(BF16)
    h2 = hmid + _swiglu(hn2, wg_ref, wu_ref, wd_ref, fc)
    o_ref[...] = _rms(h2, gf_ref[...])


def _layer1(a, h, wo, g2, wg, wu, wd, gf, *, rows, fc):
    n, d = h.shape
    f = wg.shape[1]
    kern = functools.partial(_layer1_kernel, fc=fc)
    return pl.pallas_call(
        kern,
        out_shape=jax.ShapeDtypeStruct((n, d), F32),
        grid=(n // rows,),
        in_specs=[
            pl.BlockSpec((rows, d), lambda i: (i, 0)),
            pl.BlockSpec((rows, d), lambda i: (i, 0)),
            _resident((d, d)),
            _resident((1, d)),
            _resident((d, f)),
            _resident((d, f)),
            _resident((f, d)),
            _resident((1, d)),
        ],
        out_specs=pl.BlockSpec((rows, d), lambda i: (i, 0)),
        compiler_params=pltpu.CompilerParams(
            dimension_semantics=("arbitrary",), vmem_limit_bytes=V7X_VMEM_LIMIT),
        name="layer1_out_swiglu_norm",
    )(a, h, wo, g2, wg, wu, wd, gf)


ATTN_HEADS_PER_STEP = 2


def _pick_rows(seq, want):
    rows = min(seq, want)
    assert seq % rows == 0 and rows % HALO == 0
    return rows


def kernel(x, meta_tokens, rel_bias, mix_norm_g, ffn_norm_g, pool_w, pool_scale, attn_w_qkv, attn_w_o,
           lambda_q1, lambda_k1, lambda_q2, lambda_k2, subln_g, ffn_w_gate, ffn_w_up, ffn_w_down,
           final_norm_g):
    bsz, seq, d = x.shape
    assert meta_tokens.shape == (N_META, d) and N_META == HALO
    rows = _pick_rows(seq, 512)
    tq = _pick_rows(seq, 512)
    assert tq >= 113
    fc = 256
    lambda_init = 0.8 - 0.6 * math.exp(-0.3 * 1)

    row = lambda v: v.reshape(1, -1)
    wg, wu, wd = (w.astype(BF16) for w in (ffn_w_gate, ffn_w_up, ffn_w_down))
    pw = pool_w[0].astype(BF16)
    wqkv = attn_w_qkv[0].astype(BF16)
    wo = attn_w_o[0].astype(BF16)
    xr = x.reshape(bsz * seq, d)

    l0 = functools.partial(_layer0, g1=row(mix_norm_g[0]), pw=pw, ps=row(pool_scale[0]),
                           g2=row(ffn_norm_g[0]), wg=wg[0], wu=wu[0], wd=wd[0], fc=fc)
    h1 = l0(xr, meta_tokens, rows=rows, tiles_per_seq=seq // rows, pos0=N_META)
    h1_meta = l0(meta_tokens, jnp.zeros_like(meta_tokens), rows=N_META, tiles_per_seq=1, pos0=0)

    qkv = _qkv(h1, row(mix_norm_g[1]), wqkv, rows=rows)
    qkv_meta = _qkv(h1_meta, row(mix_norm_g[1]), wqkv, rows=N_META)

    lam_vecs = jnp.stack([lambda_q1[0], lambda_k1[0], lambda_q2[0], lambda_k2[0]])
    bias_d, bias_m, lam = _bias_tables(rel_bias, lam_vecs, tq=tq, lambda_init=lambda_init)
    attn = _attention(qkv.reshape(bsz, seq, 3 * d), qkv_meta, lam, bias_d, bias_m,
                      row(subln_g[0]), tq=tq, lambda_init=lambda_init, heads=ATTN_HEADS_PER_STEP)

    out = _layer1(attn.reshape(bsz * seq, d), h1, wo, row(ffn_norm_g[1]), wg[1], wu[1], wd[1],
                  row(final_norm_g), rows=rows, fc=fc)
    return out.reshape(bsz, seq, d)
```

```python
import functools
import math

import numpy as np
import jax
import jax.numpy as jnp
from jax import lax
from jax.experimental import pallas as pl
from jax.experimental.pallas import tpu as pltpu

N_META = 16
POOL_WINDOWS = (2, 4, 8, 16)
HEAD_DIM = 64
V_DIM = 2 * HEAD_DIM
N_BUCKETS = 32
MAX_DISTANCE = 128
RMS_EPS = 1e-6
NEG_INF = -1e30
LOG2E = math.log2(math.e)
HALO = 16
BIAS_BLOCK = 128
V7X_VMEM_LIMIT = 56 * 1024 * 1024

F32 = jnp.float32
BF16 = jnp.bfloat16
_NT = (((1,), (1,)), ((), ()))


def _rms(x, g):
    ms = jnp.mean(x * x, axis=-1, keepdims=True)
    return x * lax.rsqrt(ms + RMS_EPS) * g


def _swiglu(hn, wg_ref, wu_ref, wd_ref, fc):
    acc = None
    for c in range(wg_ref.shape[1] // fc):
        sl = slice(c * fc, (c + 1) * fc)
        g = jnp.dot(hn, wg_ref[:, sl], preferred_element_type=F32)
        u = jnp.dot(hn, wu_ref[:, sl], preferred_element_type=F32)
        a = (g * jax.nn.sigmoid(g) * u).astype(BF16)
        d = jnp.dot(a, wd_ref[sl, :], preferred_element_type=F32)
        acc = d if acc is None else acc + d
    return acc


def _layer0_kernel(x_ref, xprev_ref, first_ref, g1_ref, pw_ref, ps_ref, g2_ref,
                   wg_ref, wu_ref, wd_ref, o_ref, *, tiles_per_seq, pos0, fc):
    i = pl.program_id(0)
    rows = x_ref.shape[0]
    gc = pw_ref.shape[1]
    tile_in_seq = i % tiles_per_seq
    x = x_ref[...]
    g1 = g1_ref[...]
    halo_raw = jnp.where(tile_in_seq == 0, first_ref[...], xprev_ref[...])
    hn = _rms(x, g1)
    full = jnp.concatenate([_rms(halo_raw, g1), hn], axis=0)
    pos = lax.broadcasted_iota(jnp.int32, (rows, 1), 0) + tile_in_seq * rows + pos0
    mixes = []
    for g, w in enumerate(POOL_WINDOWS):
        cols = slice(g * gc, (g + 1) * gc)
        s = full[:, cols]
        k = 1
        while k < w:
            s = s + pltpu.roll(s, k, 0)
            k *= 2
        inv_cnt = 1.0 / jnp.minimum(pos + 1, w).astype(F32)
        pooled = s[HALO:] * inv_cnt - hn[:, cols]
        mixes.append(jnp.dot(pooled.astype(BF16), pw_ref[g], preferred_element_type=F32))
    hmid = x + jnp.concatenate(mixes, axis=-1) * ps_ref[...]
    hn2 = _rms(hmid, g2_ref[...]).astype(BF16)
    o_ref[...] = hmid + _swiglu(hn2, wg_ref, wu_ref, wd_ref, fc)


def _resident(shape):
    return pl.BlockSpec(shape, lambda *_: (0,) * len(shape), pipeline_mode=pl.Buffered(1))


def _layer0(x, first, g1, pw, ps, g2, wg, wu, wd, *, rows, tiles_per_seq, pos0, fc):
    n, d = x.shape
    f = wg.shape[1]
    halo_blocks = rows // HALO
    kern = functools.partial(_layer0_kernel, tiles_per_seq=tiles_per_seq, pos0=pos0, fc=fc)
    return pl.pallas_call(
        kern,
        out_shape=jax.ShapeDtypeStruct((n, d), F32),
        grid=(n // rows,),
        in_specs=[
            pl.BlockSpec((rows, d), lambda i: (i, 0)),
            pl.BlockSpec((HALO, d), lambda i: (jnp.maximum(i * halo_blocks - 1, 0), 0)),
            _resident((HALO, d)),
            _resident((1, d)),
            _resident(pw.shape),
            _resident((1, d)),
            _resident((1, d)),
            _resident((d, f)),
            _resident((d, f)),
            _resident((f, d)),
        ],
        out_specs=pl.BlockSpec((rows, d), lambda i: (i, 0)),
        compiler_params=pltpu.CompilerParams(
            dimension_semantics=("arbitrary",), vmem_limit_bytes=V7X_VMEM_LIMIT),
        name="layer0_pool_swiglu",
    )(x, x, first, g1, pw, ps, g2, wg, wu, wd)


def _qkv_kernel(h_ref, g_ref, w_ref, o_ref):
    d = h_ref.shape[1]
    hn = _rms(h_ref[...], g_ref[...]).astype(BF16)
    for c in range(3):
        sl = slice(c * d, (c + 1) * d)
        y = jnp.dot(hn, w_ref[:, sl], preferred_element_type=F32)
        if c == 0:
            y = y * (LOG2E / math.sqrt(HEAD_DIM))
        o_ref[:, sl] = y.astype(BF16)


def _qkv(h, g, w, *, rows):
    n, d = h.shape
    return pl.pallas_call(
        _qkv_kernel,
        out_shape=jax.ShapeDtypeStruct((n, 3 * d), BF16),
        grid=(n // rows,),
        in_specs=[
            pl.BlockSpec((rows, d), lambda i: (i, 0)),
            _resident((1, d)),
            _resident((d, 3 * d)),
        ],
        out_specs=pl.BlockSpec((rows, 3 * d), lambda i: (i, 0)),
        compiler_params=pltpu.CompilerParams(
            dimension_semantics=("arbitrary",), vmem_limit_bytes=V7X_VMEM_LIMIT),
        name="qkv_proj",
    )(h, g, w)


def _bucket_table(rel):
    n = np.maximum(rel, 0)
    max_exact = N_BUCKETS // 2
    nf = np.maximum(n, max_exact).astype(np.float32)
    large = max_exact + (np.log(nf / np.float32(max_exact)) / np.float32(math.log(MAX_DISTANCE / max_exact))
                         * np.float32(N_BUCKETS - max_exact)).astype(np.int32)
    large = np.minimum(large, N_BUCKETS - 1)
    return np.where(rel < 0, -1, np.where(n < max_exact, n, large)).astype(np.int32)


def _bias_kernel(rb_ref, lv_ref, idxd_ref, idxm_ref, bd_ref, bm_ref, lam_ref, *, lambda_init):
    h = pl.program_id(0)
    far = rb_ref[N_BUCKETS - 1, h]

    def build(idx):
        out = jnp.full(idx.shape, NEG_INF, F32)
        for b in range(N_BUCKETS):
            out = jnp.where(idx == b, (rb_ref[b, h] - far) * LOG2E, out)
        return out

    near = [build(idxd_ref[e]) for e in range(2)]
    const = {True: jnp.full((BIAS_BLOCK, BIAS_BLOCK), NEG_INF, F32),
             False: jnp.zeros((BIAS_BLOCK, BIAS_BLOCK), F32)}
    blocks = bd_ref.shape[1] // BIAS_BLOCK
    for d in range(bd_ref.shape[0]):
        for a in range(blocks):
            for c in range(blocks):
                e = d * blocks + a - c
                bd_ref[d, a * BIAS_BLOCK:(a + 1) * BIAS_BLOCK, c * BIAS_BLOCK:(c + 1) * BIAS_BLOCK] = (
                    near[e] if 0 <= e < 2 else const[e < 0])
    bm_ref[...] = jnp.zeros(bm_ref.shape, F32)
    bm_ref[:BIAS_BLOCK, :] = build(idxm_ref[...])
    lv = lv_ref[...]
    e1 = jnp.exp(jnp.sum(lv[0:1] * lv[1:2], axis=-1, keepdims=True))
    e2 = jnp.exp(jnp.sum(lv[2:3] * lv[3:4], axis=-1, keepdims=True))
    lam_ref[...] = e1 - e2 + lambda_init


def _bias_tables(rel_bias, lam_vecs, *, tq, lambda_init):
    n_heads = rel_bias.shape[1]
    assert tq % BIAS_BLOCK == 0
    qi = np.arange(BIAS_BLOCK)[:, None]
    idx_d = np.stack([_bucket_table(e * BIAS_BLOCK + qi - np.arange(BIAS_BLOCK)[None, :]) for e in range(2)])
    idx_m = _bucket_table(N_META + qi - np.arange(N_META)[None, :])
    kern = functools.partial(_bias_kernel, lambda_init=lambda_init)
    return pl.pallas_call(
        kern,
        out_shape=(jax.ShapeDtypeStruct((n_heads, 2, tq, tq), F32),
                   jax.ShapeDtypeStruct((n_heads, tq, N_META), F32),
                   jax.ShapeDtypeStruct((1, 1), F32)),
        grid=(n_heads,),
        in_specs=[
            pl.BlockSpec(memory_space=pltpu.SMEM),
            pl.BlockSpec(lam_vecs.shape, lambda h: (0, 0)),
            pl.BlockSpec(idx_d.shape, lambda h: (0, 0, 0)),
            pl.BlockSpec(idx_m.shape, lambda h: (0, 0)),
        ],
        out_specs=(pl.BlockSpec((None, 2, tq, tq), lambda h: (h, 0, 0, 0)),
                   pl.BlockSpec((None, tq, N_META), lambda h: (h, 0, 0)),
                   pl.BlockSpec((1, 1), lambda h: (0, 0))),
        compiler_params=pltpu.CompilerParams(dimension_semantics=("arbitrary",)),
        name="rel_bias_tiles",
    )(rel_bias, lam_vecs, jnp.asarray(idx_d), jnp.asarray(idx_m))


def _attn_kernel(lam_ref, q_ref, k_ref, v_ref, km_ref, vm_ref, bd_ref, bm_ref, g_ref,
                 o_ref, vext_ref, qs_ref, s_ref, p_ref, m_ref, acc_ref, *, tq, lambda_init, heads):
    n_q = q_ref.shape[0] // tq
    ones = jnp.ones((1, V_DIM), BF16)
    lanes = [slice(u * V_DIM, (u + 1) * V_DIM) for u in range(heads)]
    lane = lax.broadcasted_iota(jnp.int32, (tq, V_DIM), 1)

    for u in range(heads):
        vext_ref[u, :, :V_DIM] = v_ref[:, lanes[u]]
        vext_ref[u, :, V_DIM:] = jnp.broadcast_to(ones, (v_ref.shape[0], V_DIM))

    def both(b):
        return jnp.concatenate([b, b], axis=0)

    def row_max(s):
        return jnp.broadcast_to(jnp.max(s, axis=-1, keepdims=True), (s.shape[0], V_DIM))

    def wide(r, like):
        return jnp.concatenate([r] * (like.shape[1] // V_DIM), axis=1)

    def scores(u, tile):
        off = pl.multiple_of(tile * tq, tq)
        return lax.dot_general(qs_ref[u], k_ref[pl.ds(off, tq), lanes[u]], _NT, preferred_element_type=F32)

    def weighted_values(u, tile):
        off = pl.multiple_of(tile * tq, tq)
        return jnp.dot(p_ref[u], vext_ref[u, pl.ds(off, tq), :], preferred_element_type=F32)

    def start(i):
        rows = pl.ds(pl.multiple_of(i * tq, tq), tq)
        for u in range(heads):
            q = q_ref[rows, lanes[u]]
            zero = jnp.zeros_like(q)
            qs_ref[u] = jnp.concatenate([jnp.where(lane < HEAD_DIM, q, zero),
                                         jnp.where(lane >= HEAD_DIM, q, zero)], axis=0)
            s_ref[u] = scores(u, i) + both(bd_ref[u, 0])

    def first_step(i):
        for u in range(heads):
            s_next = scores(u, jnp.maximum(i - 1, 0)) + both(bd_ref[u, 1])
            s_meta = lax.dot_general(qs_ref[u], km_ref[:, lanes[u]], _NT, preferred_element_type=F32)
            s_meta = s_meta + both(jnp.where(i == 0, bm_ref[u], 0.0))
            s = s_ref[u]
            m = jnp.maximum(row_max(s), row_max(s_meta))
            p_meta = jnp.exp2(s_meta - m[:, :N_META]).astype(BF16)
            p_ref[u] = jnp.exp2(s - wide(m, s)).astype(BF16)
            m_ref[u] = m
            s_ref[u] = s_next
            vm_ext = jnp.concatenate([vm_ref[:, lanes[u]], jnp.broadcast_to(ones, (N_META, V_DIM))], axis=-1)
            acc_ref[u] = jnp.dot(p_meta, vm_ext, preferred_element_type=F32)

    def step(next_tile, prev_tile):
        for u in range(heads):
            s_next = scores(u, next_tile)
            pv = weighted_values(u, prev_tile)
            s = s_ref[u]
            m_old = m_ref[u]
            m_new = jnp.maximum(m_old, row_max(s))
            p_ref[u] = jnp.exp2(s - wide(m_new, s)).astype(BF16)
            m_ref[u] = m_new
            s_ref[u] = s_next
            acc_ref[u] = (acc_ref[u] + pv) * wide(jnp.exp2(m_old - m_new), pv)

    def finish(i):
        rows = pl.ds(pl.multiple_of(i * tq, tq), tq)
        for u in range(heads):
            acc = acc_ref[u] + weighted_values(u, 0)
            o = acc[:, :V_DIM] / acc[:, V_DIM:]
            o = o[:tq] - lam_ref[0, 0] * o[tq:]
            o = _rms(o, g_ref[...]) * (1.0 - lambda_init)
            o_ref[rows, lanes[u]] = o.astype(BF16)

    start(0)

    @pl.loop(0, n_q)
    def _(i):
        first_step(i)

        @pl.loop(1, i + 1)
        def _(t):
            step(jnp.maximum(i - t - 1, 0), i - t + 1)

        finish(i)
        start(jnp.minimum(i + 1, n_q - 1))


def _attention(qkv, qkv_meta, lam, bias_d, bias_m, subln_g, *, tq, lambda_init, heads):
    b, s, d3 = qkv.shape
    d = d3 // 3
    n_groups = d // (V_DIM * heads)
    w = V_DIM * heads
    kern = functools.partial(_attn_kernel, tq=tq, lambda_init=lambda_init, heads=heads)
    return pl.pallas_call(
        kern,
        out_shape=jax.ShapeDtypeStruct((b, s, d), BF16),
        grid=(b, n_groups),
        in_specs=[
            pl.BlockSpec(memory_space=pltpu.SMEM),
            pl.BlockSpec((None, s, w), lambda b, g: (b, 0, g)),
            pl.BlockSpec((None, s, w), lambda b, g: (b, 0, n_groups + g)),
            pl.BlockSpec((None, s, w), lambda b, g: (b, 0, 2 * n_groups + g)),
            pl.BlockSpec((N_META, w), lambda b, g: (0, n_groups + g)),
            pl.BlockSpec((N_META, w), lambda b, g: (0, 2 * n_groups + g)),
            pl.BlockSpec((heads, 2, tq, tq), lambda b, g: (g, 0, 0, 0)),
            pl.BlockSpec((heads, tq, N_META), lambda b, g: (g, 0, 0)),
            pl.BlockSpec((1, V_DIM), lambda b, g: (0, 0)),
        ],
        out_specs=pl.BlockSpec((None, s, w), lambda b, g: (b, 0, g)),
        scratch_shapes=[
            pltpu.VMEM((heads, s, 2 * V_DIM), BF16),
            pltpu.VMEM((heads, 2 * tq, V_DIM), BF16),
            pltpu.VMEM((heads, 2 * tq, tq), F32),
            pltpu.VMEM((heads, 2 * tq, tq), BF16),
            pltpu.VMEM((heads, 2 * tq, V_DIM), F32),
            pltpu.VMEM((heads, 2 * tq, 2 * V_DIM), F32),
        ],
        compiler_params=pltpu.CompilerParams(
            dimension_semantics=("arbitrary", "arbitrary"),
            vmem_limit_bytes=V7X_VMEM_LIMIT),
        name="diff_attention",
    )(lam, qkv, qkv, qkv, qkv_meta, qkv_meta, bias_d, bias_m, subln_g)


def _layer1_kernel(a_ref, h_ref, wo_ref, g2_ref, wg_ref, wu_ref, wd_ref, gf_ref, o_ref, *, fc):
    hmid = h_ref[...] + jnp.dot(a_ref[...], wo_ref[...], preferred_element_type=F32)
    hn2 = _rms(hmid, g2_ref[...]).astype(BF16)
    h2 = hmid + _swiglu(hn2, wg_ref, wu_ref, wd_ref, fc)
    o_ref[...] = _rms(h2, gf_ref[...])


def _layer1(a, h, wo, g2, wg, wu, wd, gf, *, rows, fc):
    n, d = h.shape
    f = wg.shape[1]
    kern = functools.partial(_layer1_kernel, fc=fc)
    return pl.pallas_call(
        kern,
        out_shape=jax.ShapeDtypeStruct((n, d), F32),
        grid=(n // rows,),
        in_specs=[
            pl.BlockSpec((rows, d), lambda i: (i, 0)),
            pl.BlockSpec((rows, d), lambda i: (i, 0)),
            _resident((d, d)),
            _resident((1, d)),
            _resident((d, f)),
            _resident((d, f)),
            _resident((f, d)),
            _resident((1, d)),
        ],
        out_specs=pl.BlockSpec((rows, d), lambda i: (i, 0)),
        compiler_params=pltpu.CompilerParams(
            dimension_semantics=("arbitrary",), vmem_limit_bytes=V7X_VMEM_LIMIT),
        name="layer1_out_swiglu_norm",
    )(a, h, wo, g2, wg, wu, wd, gf)


ATTN_HEADS_PER_STEP = 2


def _pick_rows(seq, want):
    rows = min(seq, want)
    assert seq % rows == 0 and rows % HALO == 0
    return rows


def kernel(x, meta_tokens, rel_bias, mix_norm_g, ffn_norm_g, pool_w, pool_scale, attn_w_qkv, attn_w_o,
           lambda_q1, lambda_k1, lambda_q2, lambda_k2, subln_g, ffn_w_gate, ffn_w_up, ffn_w_down,
           final_norm_g):
    bsz, seq, d = x.shape
    assert meta_tokens.shape == (N_META, d) and N_META == HALO
    rows = _pick_rows(seq, 512)
    tq = _pick_rows(seq, 512)
    fc = 256
    lambda_init = 0.8 - 0.6 * math.exp(-0.3 * 1)

    row = lambda v: v.reshape(1, -1)
    wg, wu, wd = (w.astype(BF16) for w in (ffn_w_gate, ffn_w_up, ffn_w_down))
    pw = pool_w[0].astype(BF16)
    wqkv = attn_w_qkv[0].astype(BF16)
    wo = attn_w_o[0].astype(BF16)
    xr = x.reshape(bsz * seq, d)

    l0 = functools.partial(_layer0, g1=row(mix_norm_g[0]), pw=pw, ps=row(pool_scale[0]),
                           g2=row(ffn_norm_g[0]), wg=wg[0], wu=wu[0], wd=wd[0], fc=fc)
    h1 = l0(xr, meta_tokens, rows=rows, tiles_per_seq=seq // rows, pos0=N_META)
    h1_meta = l0(meta_tokens, jnp.zeros_like(meta_tokens), rows=N_META, tiles_per_seq=1, pos0=0)

    qkv = _qkv(h1, row(mix_norm_g[1]), wqkv, rows=rows)
    qkv_meta = _qkv(h1_meta, row(mix_norm_g[1]), wqkv, rows=N_META)

    lam_vecs = jnp.stack([lambda_q1[0], lambda_k1[0], lambda_q2[0], lambda_k2[0]])
    bias_d, bias_m, lam = _bias_tables(rel_bias, lam_vecs, tq=tq, lambda_init=lambda_init)
    attn = _attention(qkv.reshape(bsz, seq, 3 * d), qkv_meta, lam, bias_d, bias_m,
                      row(subln_g[0]), tq=tq, lambda_init=lambda_init, heads=ATTN_HEADS_PER_STEP)

    out = _layer1(attn.reshape(bsz * seq, d), h1, wo, row(ffn_norm_g[1]), wg[1], wu[1], wd[1],
                  row(final_norm_g), rows=rows, fc=fc)
    return out.reshape(bsz, seq, d)
```

```python
import functools
import math

import numpy as np
import jax
import jax.numpy as jnp
from jax import lax
from jax.experimental import pallas as pl
from jax.experimental.pallas import tpu as pltpu

N_META = 16
POOL_WINDOWS = (2, 4, 8, 16)
HEAD_DIM = 64
V_DIM = 2 * HEAD_DIM
N_BUCKETS = 32
MAX_DISTANCE = 128
RMS_EPS = 1e-6
NEG_INF = -1e30
LOG2E = math.log2(math.e)
HALO = 16
BIAS_BLOCK = 128
V7X_VMEM_LIMIT = 56 * 1024 * 1024

F32 = jnp.float32
BF16 = jnp.bfloat16
_NT = (((1,), (1,)), ((), ()))


def _rms(x, g):
    ms = jnp.mean(x * x, axis=-1, keepdims=True)
    return x * lax.rsqrt(ms + RMS_EPS) * g


def _swiglu(hn, wg_ref, wu_ref, wd_ref, fc):
    acc = None
    for c in range(wg_ref.shape[1] // fc):
        sl = slice(c * fc, (c + 1) * fc)
        g = jnp.dot(hn, wg_ref[:, sl], preferred_element_type=F32)
        u = jnp.dot(hn, wu_ref[:, sl], preferred_element_type=F32)
        a = (g * jax.nn.sigmoid(g) * u).astype(BF16)
        d = jnp.dot(a, wd_ref[sl, :], preferred_element_type=F32)
        acc = d if acc is None else acc + d
    return acc


def _layer0_kernel(x_ref, xprev_ref, first_ref, g1_ref, pw_ref, ps_ref, g2_ref,
                   wg_ref, wu_ref, wd_ref, o_ref, *, tiles_per_seq, pos0, fc):
    i = pl.program_id(0)
    rows = x_ref.shape[0]
    gc = pw_ref.shape[1]
    tile_in_seq = i % tiles_per_seq
    x = x_ref[...]
    g1 = g1_ref[...]
    halo_raw = jnp.where(tile_in_seq == 0, first_ref[...], xprev_ref[...])
    hn = _rms(x, g1)
    full = jnp.concatenate([_rms(halo_raw, g1), hn], axis=0)
    pos = lax.broadcasted_iota(jnp.int32, (rows, 1), 0) + tile_in_seq * rows + pos0
    mixes = []
    for g, w in enumerate(POOL_WINDOWS):
        cols = slice(g * gc, (g + 1) * gc)
        s = full[:, cols]
        k = 1
        while k < w:
            s = s + pltpu.roll(s, k, 0)
            k *= 2
        inv_cnt = 1.0 / jnp.minimum(pos + 1, w).astype(F32)
        pooled = s[HALO:] * inv_cnt - hn[:, cols]
        mixes.append(jnp.dot(pooled.astype(BF16), pw_ref[g], preferred_element_type=F32))
    hmid = x + jnp.concatenate(mixes, axis=-1) * ps_ref[...]
    hn2 = _rms(hmid, g2_ref[...]).astype(BF16)
    o_ref[...] = hmid + _swiglu(hn2, wg_ref, wu_ref, wd_ref, fc)


def _resident(shape):
    return pl.BlockSpec(shape, lambda *_: (0,) * len(shape), pipeline_mode=pl.Buffered(1))


def _layer0(x, first, g1, pw, ps, g2, wg, wu, wd, *, rows, tiles_per_seq, pos0, fc):
    n, d = x.shape
    f = wg.shape[1]
    halo_blocks = rows // HALO
    kern = functools.partial(_layer0_kernel, tiles_per_seq=tiles_per_seq, pos0=pos0, fc=fc)
    return pl.pallas_call(
        kern,
        out_shape=jax.ShapeDtypeStruct((n, d), F32),
        grid=(n // rows,),
        in_specs=[
            pl.BlockSpec((rows, d), lambda i: (i, 0)),
            pl.BlockSpec((HALO, d), lambda i: (jnp.maximum(i * halo_blocks - 1, 0), 0)),
            _resident((HALO, d)),
            _resident((1, d)),
            _resident(pw.shape),
            _resident((1, d)),
            _resident((1, d)),
            _resident((d, f)),
            _resident((d, f)),
            _resident((f, d)),
        ],
        out_specs=pl.BlockSpec((rows, d), lambda i: (i, 0)),
        compiler_params=pltpu.CompilerParams(
            dimension_semantics=("arbitrary",), vmem_limit_bytes=V7X_VMEM_LIMIT),
        name="layer0_pool_swiglu",
    )(x, x, first, g1, pw, ps, g2, wg, wu, wd)


def _qkv_kernel(h_ref, g_ref, w_ref, o_ref):
    d = h_ref.shape[1]
    hn = _rms(h_ref[...], g_ref[...]).astype(BF16)
    for c in range(3):
        sl = slice(c * d, (c + 1) * d)
        y = jnp.dot(hn, w_ref[:, sl], preferred_element_type=F32)
        if c == 0:
            y = y * (LOG2E / math.sqrt(HEAD_DIM))
        o_ref[:, sl] = y.astype(BF16)


def _qkv(h, g, w, *, rows):
    n, d = h.shape
    return pl.pallas_call(
        _qkv_kernel,
        out_shape=jax.ShapeDtypeStruct((n, 3 * d), BF16),
        grid=(n // rows,),
        in_specs=[
            pl.BlockSpec((rows, d), lambda i: (i, 0)),
            _resident((1, d)),
            _resident((d, 3 * d)),
        ],
        out_specs=pl.BlockSpec((rows, 3 * d), lambda i: (i, 0)),
        compiler_params=pltpu.CompilerParams(
            dimension_semantics=("arbitrary",), vmem_limit_bytes=V7X_VMEM_LIMIT),
        name="qkv_proj",
    )(h, g, w)


def _bucket_table(rel):
    n = np.maximum(rel, 0)
    max_exact = N_BUCKETS // 2
    nf = np.maximum(n, max_exact).astype(np.float32)
    large = max_exact + (np.log(nf / np.float32(max_exact)) / np.float32(math.log(MAX_DISTANCE / max_exact))
                         * np.float32(N_BUCKETS - max_exact)).astype(np.int32)
    large = np.minimum(large, N_BUCKETS - 1)
    return np.where(rel < 0, -1, np.where(n < max_exact, n, large)).astype(np.int32)


def _bias_kernel(rb_ref, lv_ref, idxd_ref, idxm_ref, bd_ref, bm_ref, lam_ref, *, lambda_init):
    h = pl.program_id(0)
    far = rb_ref[N_BUCKETS - 1, h]

    def build(idx):
        out = jnp.full(idx.shape, NEG_INF, F32)
        for b in range(N_BUCKETS):
            out = jnp.where(idx == b, (rb_ref[b, h] - far) * LOG2E, out)
        return out

    near = [build(idxd_ref[e]) for e in range(2)]
    const = {True: jnp.full((BIAS_BLOCK, BIAS_BLOCK), NEG_INF, F32),
             False: jnp.zeros((BIAS_BLOCK, BIAS_BLOCK), F32)}
    blocks = bd_ref.shape[1] // BIAS_BLOCK
    for d in range(bd_ref.shape[0]):
        for a in range(blocks):
            for c in range(blocks):
                e = d * blocks + a - c
                bd_ref[d, a * BIAS_BLOCK:(a + 1) * BIAS_BLOCK, c * BIAS_BLOCK:(c + 1) * BIAS_BLOCK] = (
                    near[e] if 0 <= e < 2 else const[e < 0])
    bm_ref[...] = jnp.zeros(bm_ref.shape, F32)
    bm_ref[:BIAS_BLOCK, :] = build(idxm_ref[...])
    lv = lv_ref[...]
    e1 = jnp.exp(jnp.sum(lv[0:1] * lv[1:2], axis=-1, keepdims=True))
    e2 = jnp.exp(jnp.sum(lv[2:3] * lv[3:4], axis=-1, keepdims=True))
    lam_ref[...] = e1 - e2 + lambda_init


def _bias_tables(rel_bias, lam_vecs, *, tq, lambda_init):
    n_heads = rel_bias.shape[1]
    assert tq % BIAS_BLOCK == 0
    qi = np.arange(BIAS_BLOCK)[:, None]
    idx_d = np.stack([_bucket_table(e * BIAS_BLOCK + qi - np.arange(BIAS_BLOCK)[None, :]) for e in range(2)])
    idx_m = _bucket_table(N_META + qi - np.arange(N_META)[None, :])
    kern = functools.partial(_bias_kernel, lambda_init=lambda_init)
    return pl.pallas_call(
        kern,
        out_shape=(jax.ShapeDtypeStruct((n_heads, 2, tq, tq), F32),
                   jax.ShapeDtypeStruct((n_heads, tq, N_META), F32),
                   jax.ShapeDtypeStruct((1, 1), F32)),
        grid=(n_heads,),
        in_specs=[
            pl.BlockSpec(memory_space=pltpu.SMEM),
            pl.BlockSpec(lam_vecs.shape, lambda h: (0, 0)),
            pl.BlockSpec(idx_d.shape, lambda h: (0, 0, 0)),
            pl.BlockSpec(idx_m.shape, lambda h: (0, 0)),
        ],
        out_specs=(pl.BlockSpec((None, 2, tq, tq), lambda h: (h, 0, 0, 0)),
                   pl.BlockSpec((None, tq, N_META), lambda h: (h, 0, 0)),
                   pl.BlockSpec((1, 1), lambda h: (0, 0))),
        compiler_params=pltpu.CompilerParams(dimension_semantics=("arbitrary",)),
        name="rel_bias_tiles",
    )(rel_bias, lam_vecs, jnp.asarray(idx_d), jnp.asarray(idx_m))


def _attn_kernel(lam_ref, q_ref, k_ref, v_ref, km_ref, vm_ref, bd_ref, bm_ref, g_ref,
                 o_ref, vext_ref, qs_ref, s_ref, p_ref, m_ref, acc_ref, *, tq, lambda_init, heads):
    n_q = q_ref.shape[0] // tq
    ones = jnp.ones((1, V_DIM), BF16)
    lanes = [slice(u * V_DIM, (u + 1) * V_DIM) for u in range(heads)]
    lane = lax.broadcasted_iota(jnp.int32, (tq, V_DIM), 1)

    for u in range(heads):
        vext_ref[u, :, :V_DIM] = v_ref[:, lanes[u]]
        vext_ref[u, :, V_DIM:] = jnp.broadcast_to(ones, (v_ref.shape[0], V_DIM))

    def both(b):
        return jnp.concatenate([b, b], axis=0)

    def row_max(s):
        return jnp.broadcast_to(jnp.max(s, axis=-1, keepdims=True), (s.shape[0], V_DIM))

    def wide(r, like):
        return jnp.concatenate([r] * (like.shape[1] // V_DIM), axis=1)

    def scores(u, tile):
        off = pl.multiple_of(tile * tq, tq)
        return lax.dot_general(qs_ref[u], k_ref[pl.ds(off, tq), lanes[u]], _NT, preferred_element_type=F32)

    def weighted_values(u, tile):
        off = pl.multiple_of(tile * tq, tq)
        return jnp.dot(p_ref[u], vext_ref[u, pl.ds(off, tq), :], preferred_element_type=F32)

    def start(i):
        rows = pl.ds(pl.multiple_of(i * tq, tq), tq)
        for u in range(heads):
            q = q_ref[rows, lanes[u]]
            zero = jnp.zeros_like(q)
            qs_ref[u] = jnp.concatenate([jnp.where(lane < HEAD_DIM, q, zero),
                                         jnp.where(lane >= HEAD_DIM, q, zero)], axis=0)
            s_ref[u] = scores(u, i) + both(bd_ref[u, 0])

    def first_step(i):
        for u in range(heads):
            s_next = scores(u, jnp.maximum(i - 1, 0)) + both(bd_ref[u, 1])
            s_meta = lax.dot_general(qs_ref[u], km_ref[:, lanes[u]], _NT, preferred_element_type=F32)
            s_meta = s_meta + both(jnp.where(i == 0, bm_ref[u], 0.0))
            s = s_ref[u]
            m = jnp.maximum(row_max(s), row_max(s_meta))
            p_meta = jnp.exp2(s_meta - m[:, :N_META]).astype(BF16)
            p_ref[u] = jnp.exp2(s - wide(m, s)).astype(BF16)
            m_ref[u] = m
            s_ref[u] = s_next
            vm_ext = jnp.concatenate([vm_ref[:, lanes[u]], jnp.broadcast_to(ones, (N_META, V_DIM))], axis=-1)
            acc_ref[u] = jnp.dot(p_meta, vm_ext, preferred_element_type=F32)

    def step(next_tile, prev_tile):
        for u in range(heads):
            s_next = scores(u, next_tile)
            pv = weighted_values(u, prev_tile)
            s = s_ref[u]
            m_old = m_ref[u]
            m_new = jnp.maximum(m_old, row_max(s))
            p_ref[u] = jnp.exp2(s - wide(m_new, s)).astype(BF16)
            m_ref[u] = m_new
            s_ref[u] = s_next
            acc_ref[u] = (acc_ref[u] + pv) * wide(jnp.exp2(m_old - m_new), pv)

    def finish(i, last_step):
        rows = pl.ds(pl.multiple_of(i * tq, tq), tq)
        for u in range(heads):
            if last_step:
                pv = weighted_values(u, 1)
                s = s_ref[u]
                m_old = m_ref[u]
                m_new = jnp.maximum(m_old, row_max(s))
                p = jnp.exp2(s - wide(m_new, s)).astype(BF16)
                acc = (acc_ref[u] + pv) * wide(jnp.exp2(m_old - m_new), pv)
                acc = acc + jnp.dot(p, vext_ref[u, :tq, :], preferred_element_type=F32)
            else:
                acc = acc_ref[u] + weighted_values(u, 0)
            o = acc[:, :V_DIM] / acc[:, V_DIM:]
            o = o[:tq] - lam_ref[0, 0] * o[tq:]
            o = _rms(o, g_ref[...]) * (1.0 - lambda_init)
            o_ref[rows, lanes[u]] = o.astype(BF16)

    start(0)

    @pl.loop(0, n_q)
    def _(i):
        first_step(i)

        @pl.loop(1, i)
        def _(t):
            step(i - t - 1, i - t + 1)

        for last_step in (False, True):
            @pl.when((i >= 1) == last_step)
            def _():
                finish(i, last_step)
                start(jnp.minimum(i + 1, n_q - 1))


def _attention(qkv, qkv_meta, lam, bias_d, bias_m, subln_g, *, tq, lambda_init, heads):
    b, s, d3 = qkv.shape
    d = d3 // 3
    n_groups = d // (V_DIM * heads)
    w = V_DIM * heads
    kern = functools.partial(_attn_kernel, tq=tq, lambda_init=lambda_init, heads=heads)
    return pl.pallas_call(
        kern,
        out_shape=jax.ShapeDtypeStruct((b, s, d), BF16),
        grid=(b, n_groups),
        in_specs=[
            pl.BlockSpec(memory_space=pltpu.SMEM),
            pl.BlockSpec((None, s, w), lambda b, g: (b, 0, g)),
            pl.BlockSpec((None, s, w), lambda b, g: (b, 0, n_groups + g)),
            pl.BlockSpec((None, s, w), lambda b, g: (b, 0, 2 * n_groups + g)),
            pl.BlockSpec((N_META, w), lambda b, g: (0, n_groups + g)),
            pl.BlockSpec((N_META, w), lambda b, g: (0, 2 * n_groups + g)),
            pl.BlockSpec((heads, 2, tq, tq), lambda b, g: (g, 0, 0, 0)),
            pl.BlockSpec((heads, tq, N_META), lambda b, g: (g, 0, 0)),
            pl.BlockSpec((1, V_DIM), lambda b, g: (0, 0)),
        ],
        out_specs=pl.BlockSpec((None, s, w), lambda b, g: (b, 0, g)),
        scratch_shapes=[
            pltpu.VMEM((heads, s, 2 * V_DIM), BF16),
            pltpu.VMEM((heads, 2 * tq, V_DIM), BF16),
            pltpu.VMEM((heads, 2 * tq, tq), F32),
            pltpu.VMEM((heads, 2 * tq, tq), BF16),
            pltpu.VMEM((heads, 2 * tq, V_DIM), F32),
            pltpu.VMEM((heads, 2 * tq, 2 * V_DIM), F32),
        ],
        compiler_params=pltpu.CompilerParams(
            dimension_semantics=("arbitrary", "arbitrary"),
            vmem_limit_bytes=V7X_VMEM_LIMIT),
        name="diff_attention",
    )(lam, qkv, qkv, qkv, qkv_meta, qkv_meta, bias_d, bias_m, subln_g)


def _layer1_kernel(a_ref, h_ref, wo_ref, g2_ref, wg_ref, wu_ref, wd_ref, gf_ref, o_ref, *, fc):
    hmid = h_ref[...] + jnp.dot(a_ref[...], wo_ref[...], preferred_element_type=F32)
    hn2 = _rms(hmid, g2_ref[...]).astype(BF16)
    h2 = hmid + _swiglu(hn2, wg_ref, wu_ref, wd_ref, fc)
    o_ref[...] = _rms(h2, gf_ref[...])


def _layer1(a, h, wo, g2, wg, wu, wd, gf, *, rows, fc):
    n, d = h.shape
    f = wg.shape[1]
    kern = functools.partial(_layer1_kernel, fc=fc)
    return pl.pallas_call(
        kern,
        out_shape=jax.ShapeDtypeStruct((n, d), F32),
        grid=(n // rows,),
        in_specs=[
            pl.BlockSpec((rows, d), lambda i: (i, 0)),
            pl.BlockSpec((rows, d), lambda i: (i, 0)),
            _resident((d, d)),
            _resident((1, d)),
            _resident((d, f)),
            _resident((d, f)),
            _resident((f, d)),
            _resident((1, d)),
        ],
        out_specs=pl.BlockSpec((rows, d), lambda i: (i, 0)),
        compiler_params=pltpu.CompilerParams(
            dimension_semantics=("arbitrary",), vmem_limit_bytes=V7X_VMEM_LIMIT),
        name="layer1_out_swiglu_norm",
    )(a, h, wo, g2, wg, wu, wd, gf)


ATTN_HEADS_PER_STEP = 2


def _pick_rows(seq, want):
    rows = min(seq, want)
    assert seq % rows == 0 and rows % HALO == 0
    return rows


def kernel(x, meta_tokens, rel_bias, mix_norm_g, ffn_norm_g, pool_w, pool_scale, attn_w_qkv, attn_w_o,
           lambda_q1, lambda_k1, lambda_q2, lambda_k2, subln_g, ffn_w_gate, ffn_w_up, ffn_w_down,
           final_norm_g):
    bsz, seq, d = x.shape
    assert meta_tokens.shape == (N_META, d) and N_META == HALO
    rows = _pick_rows(seq, 512)
    tq = _pick_rows(seq, 512)
    fc = 256
    lambda_init = 0.8 - 0.6 * math.exp(-0.3 * 1)

    row = lambda v: v.reshape(1, -1)
    wg, wu, wd = (w.astype(BF16) for w in (ffn_w_gate, ffn_w_up, ffn_w_down))
    pw = pool_w[0].astype(BF16)
    wqkv = attn_w_qkv[0].astype(BF16)
    wo = attn_w_o[0].astype(BF16)
    xr = x.reshape(bsz * seq, d)

    l0 = functools.partial(_layer0, g1=row(mix_norm_g[0]), pw=pw, ps=row(pool_scale[0]),
                           g2=row(ffn_norm_g[0]), wg=wg[0], wu=wu[0], wd=wd[0], fc=fc)
    h1 = l0(xr, meta_tokens, rows=rows, tiles_per_seq=seq // rows, pos0=N_META)
    h1_meta = l0(meta_tokens, jnp.zeros_like(meta_tokens), rows=N_META, tiles_per_seq=1, pos0=0)

    qkv = _qkv(h1, row(mix_norm_g[1]), wqkv, rows=rows)
    qkv_meta = _qkv(h1_meta, row(mix_norm_g[1]), wqkv, rows=N_META)

    lam_vecs = jnp.stack([lambda_q1[0], lambda_k1[0], lambda_q2[0], lambda_k2[0]])
    bias_d, bias_m, lam = _bias_tables(rel_bias, lam_vecs, tq=tq, lambda_init=lambda_init)
    attn = _attention(qkv.reshape(bsz, seq, 3 * d), qkv_meta, lam, bias_d, bias_m,
                      row(subln_g[0]), tq=tq, lambda_init=lambda_init, heads=ATTN_HEADS_PER_STEP)

    out = _layer1(attn.reshape(bsz * seq, d), h1, wo, row(ffn_norm_g[1]), wg[1], wu[1], wd[1],
                  row(final_norm_g), rows=rows, fc=fc)
    return out.reshape(bsz, seq, d)
```

```python
import functools
import math

import numpy as np
import jax
import jax.numpy as jnp
from jax import lax
from jax.experimental import pallas as pl
from jax.experimental.pallas import tpu as pltpu

N_META = 16
POOL_WINDOWS = (2, 4, 8, 16)
HEAD_DIM = 64
V_DIM = 2 * HEAD_DIM
N_BUCKETS = 32
MAX_DISTANCE = 128
RMS_EPS = 1e-6
NEG_INF = -1e30
LOG2E = math.log2(math.e)
HALO = 16
BIAS_BLOCK = 128
V7X_VMEM_LIMIT = 56 * 1024 * 1024

F32 = jnp.float32
BF16 = jnp.bfloat16
_NT = (((1,), (1,)), ((), ()))


def _rms(x, g):
    ms = jnp.mean(x * x, axis=-1, keepdims=True)
    return x * lax.rsqrt(ms + RMS_EPS) * g


def _swiglu(hn, wg_ref, wu_ref, wd_ref, fc):
    acc = None
    for c in range(wg_ref.shape[1] // fc):
        sl = slice(c * fc, (c + 1) * fc)
        g = jnp.dot(hn, wg_ref[:, sl], preferred_element_type=F32)
        u = jnp.dot(hn, wu_ref[:, sl], preferred_element_type=F32)
        a = (g * jax.nn.sigmoid(g) * u).astype(BF16)
        d = jnp.dot(a, wd_ref[sl, :], preferred_element_type=F32)
        acc = d if acc is None else acc + d
    return acc


def _layer0_kernel(x_ref, xprev_ref, first_ref, g1_ref, pw_ref, ps_ref, g2_ref,
                   wg_ref, wu_ref, wd_ref, o_ref, *, tiles_per_seq, pos0, fc):
    i = pl.program_id(0)
    rows = x_ref.shape[0]
    gc = pw_ref.shape[1]
    tile_in_seq = i % tiles_per_seq
    x = x_ref[...]
    g1 = g1_ref[...]
    halo_raw = jnp.where(tile_in_seq == 0, first_ref[...], xprev_ref[...])
    hn = _rms(x, g1)
    full = jnp.concatenate([_rms(halo_raw, g1), hn], axis=0)
    pos = lax.broadcasted_iota(jnp.int32, (rows, 1), 0) + tile_in_seq * rows + pos0
    mixes = []
    for g, w in enumerate(POOL_WINDOWS):
        cols = slice(g * gc, (g + 1) * gc)
        s = full[:, cols]
        k = 1
        while k < w:
            s = s + pltpu.roll(s, k, 0)
            k *= 2
        inv_cnt = 1.0 / jnp.minimum(pos + 1, w).astype(F32)
        pooled = s[HALO:] * inv_cnt - hn[:, cols]
        mixes.append(jnp.dot(pooled.astype(BF16), pw_ref[g], preferred_element_type=F32))
    hmid = x + jnp.concatenate(mixes, axis=-1) * ps_ref[...]
    hn2 = _rms(hmid, g2_ref[...]).astype(BF16)
    o_ref[...] = hmid + _swiglu(hn2, wg_ref, wu_ref, wd_ref, fc)


def _resident(shape):
    return pl.BlockSpec(shape, lambda *_: (0,) * len(shape), pipeline_mode=pl.Buffered(1))


def _layer0(x, first, g1, pw, ps, g2, wg, wu, wd, *, rows, tiles_per_seq, pos0, fc):
    n, d = x.shape
    f = wg.shape[1]
    halo_blocks = rows // HALO
    kern = functools.partial(_layer0_kernel, tiles_per_seq=tiles_per_seq, pos0=pos0, fc=fc)
    return pl.pallas_call(
        kern,
        out_shape=jax.ShapeDtypeStruct((n, d), F32),
        grid=(n // rows,),
        in_specs=[
            pl.BlockSpec((rows, d), lambda i: (i, 0)),
            pl.BlockSpec((HALO, d), lambda i: (jnp.maximum(i * halo_blocks - 1, 0), 0)),
            _resident((HALO, d)),
            _resident((1, d)),
            _resident(pw.shape),
            _resident((1, d)),
            _resident((1, d)),
            _resident((d, f)),
            _resident((d, f)),
            _resident((f, d)),
        ],
        out_specs=pl.BlockSpec((rows, d), lambda i: (i, 0)),
        compiler_params=pltpu.CompilerParams(
            dimension_semantics=("arbitrary",), vmem_limit_bytes=V7X_VMEM_LIMIT),
        name="layer0_pool_swiglu",
    )(x, x, first, g1, pw, ps, g2, wg, wu, wd)


def _qkv_kernel(h_ref, g_ref, w_ref, o_ref):
    d = h_ref.shape[1]
    hn = _rms(h_ref[...], g_ref[...]).astype(BF16)
    for c in range(3):
        sl = slice(c * d, (c + 1) * d)
        y = jnp.dot(hn, w_ref[:, sl], preferred_element_type=F32)
        if c == 0:
            y = y * (LOG2E / math.sqrt(HEAD_DIM))
        o_ref[:, sl] = y.astype(BF16)


def _qkv(h, g, w, *, rows):
    n, d = h.shape
    return pl.pallas_call(
        _qkv_kernel,
        out_shape=jax.ShapeDtypeStruct((n, 3 * d), BF16),
        grid=(n // rows,),
        in_specs=[
            pl.BlockSpec((rows, d), lambda i: (i, 0)),
            _resident((1, d)),
            _resident((d, 3 * d)),
        ],
        out_specs=pl.BlockSpec((rows, 3 * d), lambda i: (i, 0)),
        compiler_params=pltpu.CompilerParams(
            dimension_semantics=("arbitrary",), vmem_limit_bytes=V7X_VMEM_LIMIT),
        name="qkv_proj",
    )(h, g, w)


def _bucket_table(rel):
    n = np.maximum(rel, 0)
    max_exact = N_BUCKETS // 2
    nf = np.maximum(n, max_exact).astype(np.float32)
    large = max_exact + (np.log(nf / np.float32(max_exact)) / np.float32(math.log(MAX_DISTANCE / max_exact))
                         * np.float32(N_BUCKETS - max_exact)).astype(np.int32)
    large = np.minimum(large, N_BUCKETS - 1)
    return np.where(rel < 0, -1, np.where(n < max_exact, n, large)).astype(np.int32)


def _bias_kernel(rb_ref, lv_ref, idxd_ref, idxm_ref, bd_ref, bm_ref, lam_ref, *, lambda_init):
    h = pl.program_id(0)
    far = rb_ref[N_BUCKETS - 1, h]

    def build(idx):
        out = jnp.full(idx.shape, NEG_INF, F32)
        for b in range(N_BUCKETS):
            out = jnp.where(idx == b, (rb_ref[b, h] - far) * LOG2E, out)
        return out

    near = [build(idxd_ref[e]) for e in range(2)]
    const = {True: jnp.full((BIAS_BLOCK, BIAS_BLOCK), NEG_INF, F32),
             False: jnp.zeros((BIAS_BLOCK, BIAS_BLOCK), F32)}
    blocks = bd_ref.shape[1] // BIAS_BLOCK
    for d in range(bd_ref.shape[0]):
        for a in range(blocks):
            for c in range(blocks):
                e = d * blocks + a - c
                bd_ref[d, a * BIAS_BLOCK:(a + 1) * BIAS_BLOCK, c * BIAS_BLOCK:(c + 1) * BIAS_BLOCK] = (
                    near[e] if 0 <= e < 2 else const[e < 0])
    bm_ref[...] = jnp.zeros(bm_ref.shape, F32)
    bm_ref[:BIAS_BLOCK, :] = build(idxm_ref[...])
    lv = lv_ref[...]
    e1 = jnp.exp(jnp.sum(lv[0:1] * lv[1:2], axis=-1, keepdims=True))
    e2 = jnp.exp(jnp.sum(lv[2:3] * lv[3:4], axis=-1, keepdims=True))
    lam_ref[...] = e1 - e2 + lambda_init


def _bias_tables(rel_bias, lam_vecs, *, tq, lambda_init):
    n_heads = rel_bias.shape[1]
    assert tq % BIAS_BLOCK == 0
    qi = np.arange(BIAS_BLOCK)[:, None]
    idx_d = np.stack([_bucket_table(e * BIAS_BLOCK + qi - np.arange(BIAS_BLOCK)[None, :]) for e in range(2)])
    idx_m = _bucket_table(N_META + qi - np.arange(N_META)[None, :])
    kern = functools.partial(_bias_kernel, lambda_init=lambda_init)
    return pl.pallas_call(
        kern,
        out_shape=(jax.ShapeDtypeStruct((n_heads, 2, tq, tq), F32),
                   jax.ShapeDtypeStruct((n_heads, tq, N_META), F32),
                   jax.ShapeDtypeStruct((1, 1), F32)),
        grid=(n_heads,),
        in_specs=[
            pl.BlockSpec(memory_space=pltpu.SMEM),
            pl.BlockSpec(lam_vecs.shape, lambda h: (0, 0)),
            pl.BlockSpec(idx_d.shape, lambda h: (0, 0, 0)),
            pl.BlockSpec(idx_m.shape, lambda h: (0, 0)),
        ],
        out_specs=(pl.BlockSpec((None, 2, tq, tq), lambda h: (h, 0, 0, 0)),
                   pl.BlockSpec((None, tq, N_META), lambda h: (h, 0, 0)),
                   pl.BlockSpec((1, 1), lambda h: (0, 0))),
        compiler_params=pltpu.CompilerParams(dimension_semantics=("arbitrary",)),
        name="rel_bias_tiles",
    )(rel_bias, lam_vecs, jnp.asarray(idx_d), jnp.asarray(idx_m))


def _attn_kernel(lam_ref, q_ref, k_ref, v_ref, km_ref, vm_ref, bd_ref, bm_ref, g_ref,
                 o_ref, vext_ref, qs_ref, s_ref, p_ref, m_ref, acc_ref, *, tq, lambda_init, heads):
    n_q = q_ref.shape[0] // tq
    ones = jnp.ones((1, V_DIM), BF16)
    lanes = [slice(u * V_DIM, (u + 1) * V_DIM) for u in range(heads)]
    lane = lax.broadcasted_iota(jnp.int32, (tq, V_DIM), 1)

    for u in range(heads):
        vext_ref[u, :, :V_DIM] = v_ref[:, lanes[u]]
        vext_ref[u, :, V_DIM:] = jnp.broadcast_to(ones, (v_ref.shape[0], V_DIM))

    def both(b):
        return jnp.concatenate([b, b], axis=0)

    def row_max(s):
        return jnp.broadcast_to(jnp.max(s, axis=-1, keepdims=True), (s.shape[0], V_DIM))

    def wide(r, like):
        return jnp.concatenate([r] * (like.shape[1] // V_DIM), axis=1)

    def scores(u, tile):
        off = pl.multiple_of(tile * tq, tq)
        return lax.dot_general(qs_ref[u], k_ref[pl.ds(off, tq), lanes[u]], _NT, preferred_element_type=F32)

    def weighted_values(u, tile):
        off = pl.multiple_of(tile * tq, tq)
        return jnp.dot(p_ref[u], vext_ref[u, pl.ds(off, tq), :], preferred_element_type=F32)

    def start(i):
        rows = pl.ds(pl.multiple_of(i * tq, tq), tq)
        for u in range(heads):
            q = q_ref[rows, lanes[u]]
            zero = jnp.zeros_like(q)
            qs_ref[u] = jnp.concatenate([jnp.where(lane < HEAD_DIM, q, zero),
                                         jnp.where(lane >= HEAD_DIM, q, zero)], axis=0)
            s_ref[u] = scores(u, i) + both(bd_ref[u, 0])

    def first_step(i):
        for u in range(heads):
            s_next = scores(u, jnp.maximum(i - 1, 0)) + both(bd_ref[u, 1])
            s_meta = lax.dot_general(qs_ref[u], km_ref[:, lanes[u]], _NT, preferred_element_type=F32)
            s_meta = s_meta + both(jnp.where(i == 0, bm_ref[u], 0.0))
            s = s_ref[u]
            m = jnp.maximum(row_max(s), row_max(s_meta))
            p_meta = jnp.exp2(s_meta - m[:, :N_META]).astype(BF16)
            p_ref[u] = jnp.exp2(s - wide(m, s)).astype(BF16)
            m_ref[u] = m
            s_ref[u] = s_next
            vm_ext = jnp.concatenate([vm_ref[:, lanes[u]], jnp.broadcast_to(ones, (N_META, V_DIM))], axis=-1)
            acc_ref[u] = jnp.dot(p_meta, vm_ext, preferred_element_type=F32)

    def step(next_tile, prev_tile):
        for u in range(heads):
            s_next = scores(u, next_tile)
            pv = weighted_values(u, prev_tile)
            s = s_ref[u]
            m_old = m_ref[u]
            m_new = jnp.maximum(m_old, row_max(s))
            p_ref[u] = jnp.exp2(s - wide(m_new, s)).astype(BF16)
            m_ref[u] = m_new
            s_ref[u] = s_next
            acc_ref[u] = (acc_ref[u] + pv) * wide(jnp.exp2(m_old - m_new), pv)

    def finish(i, last_step):
        rows = pl.ds(pl.multiple_of(i * tq, tq), tq)
        for u in range(heads):
            if last_step:
                pv = weighted_values(u, 1)
                s = s_ref[u]
                m_old = m_ref[u]
                m_new = jnp.maximum(m_old, row_max(s))
                p = jnp.exp2(s - wide(m_new, s)).astype(BF16)
                acc = (acc_ref[u] + pv) * wide(jnp.exp2(m_old - m_new), pv)
                acc = acc + jnp.dot(p, vext_ref[u, :tq, :], preferred_element_type=F32)
            else:
                acc = acc_ref[u] + weighted_values(u, 0)
            o = acc[:, :V_DIM] / acc[:, V_DIM:]
            o = o[:tq] - lam_ref[0, 0] * o[tq:]
            o = _rms(o, g_ref[...]) * (1.0 - lambda_init)
            o_ref[rows, lanes[u]] = o.astype(BF16)

    start(0)

    @pl.loop(0, n_q)
    def _(i):
        first_step(i)

        @pl.loop(1, i)
        def _(t):
            step(i - t - 1, i - t + 1)

        for last_step in (False, True):
            @pl.when((i >= 1) == last_step)
            def _():
                finish(i, last_step)
                start(jnp.minimum(i + 1, n_q - 1))


def _attention(qkv, qkv_meta, lam, bias_d, bias_m, subln_g, *, tq, lambda_init, heads):
    b, s, d3 = qkv.shape
    d = d3 // 3
    n_groups = d // (V_DIM * heads)
    w = V_DIM * heads
    kern = functools.partial(_attn_kernel, tq=tq, lambda_init=lambda_init, heads=heads)
    return pl.pallas_call(
        kern,
        out_shape=jax.ShapeDtypeStruct((b, s, d), BF16),
        grid=(b, n_groups),
        in_specs=[
            pl.BlockSpec(memory_space=pltpu.SMEM),
            pl.BlockSpec((None, s, w), lambda b, g: (b, 0, g)),
            pl.BlockSpec((None, s, w), lambda b, g: (b, 0, n_groups + g)),
            pl.BlockSpec((None, s, w), lambda b, g: (b, 0, 2 * n_groups + g)),
            pl.BlockSpec((N_META, w), lambda b, g: (0, n_groups + g)),
            pl.BlockSpec((N_META, w), lambda b, g: (0, 2 * n_groups + g)),
            pl.BlockSpec((heads, 2, tq, tq), lambda b, g: (g, 0, 0, 0)),
            pl.BlockSpec((heads, tq, N_META), lambda b, g: (g, 0, 0)),
            pl.BlockSpec((1, V_DIM), lambda b, g: (0, 0)),
        ],
        out_specs=pl.BlockSpec((None, s, w), lambda b, g: (b, 0, g)),
        scratch_shapes=[
            pltpu.VMEM((heads, s, 2 * V_DIM), BF16),
            pltpu.VMEM((heads, 2 * tq, V_DIM), BF16),
            pltpu.VMEM((heads, 2 * tq, tq), F32),
            pltpu.VMEM((heads, 2 * tq, tq), BF16),
            pltpu.VMEM((heads, 2 * tq, V_DIM), F32),
            pltpu.VMEM((heads, 2 * tq, 2 * V_DIM), F32),
        ],
        compiler_params=pltpu.CompilerParams(
            dimension_semantics=("arbitrary", "arbitrary"),
            vmem_limit_bytes=V7X_VMEM_LIMIT),
        name="diff_attention",
    )(lam, qkv, qkv, qkv, qkv_meta, qkv_meta, bias_d, bias_m, subln_g)


def _layer1_kernel(a_ref, h_ref, wo_ref, g2_ref, wg_ref, wu_ref, wd_ref, gf_ref, o_ref, *, fc):
    hmid = h_ref[...] + jnp.dot(a_ref[...], wo_ref[...], preferred_element_type=F32)
    hn2 = _rms(hmid, g2_ref[...]).astype(BF16)
    h2 = hmid + _swiglu(hn2, wg_ref, wu_ref, wd_ref, fc)
    o_ref[...] = _rms(h2, gf_ref[...])


def _layer1(a, h, wo, g2, wg, wu, wd, gf, *, rows, fc):
    n, d = h.shape
    f = wg.shape[1]
    kern = functools.partial(_layer1_kernel, fc=fc)
    return pl.pallas_call(
        kern,
        out_shape=jax.ShapeDtypeStruct((n, d), F32),
        grid=(n // rows,),
        in_specs=[
            pl.BlockSpec((rows, d), lambda i: (i, 0)),
            pl.BlockSpec((rows, d), lambda i: (i, 0)),
            _resident((d, d)),
            _resident((1, d)),
            _resident((d, f)),
            _resident((d, f)),
            _resident((f, d)),
            _resident((1, d)),
        ],
        out_specs=pl.BlockSpec((rows, d), lambda i: (i, 0)),
        compiler_params=pltpu.CompilerParams(
            dimension_semantics=("arbitrary",), vmem_limit_bytes=V7X_VMEM_LIMIT),
        name="layer1_out_swiglu_norm",
    )(a, h, wo, g2, wg, wu, wd, gf)


ATTN_HEADS_PER_STEP = 2


def _pick_rows(seq, want):
    rows = min(seq, want)
    assert seq % rows == 0 and rows % HALO == 0
    return rows


def kernel(x, meta_tokens, rel_bias, mix_norm_g, ffn_norm_g, pool_w, pool_scale, attn_w_qkv, attn_w_o,
           lambda_q1, lambda_k1, lambda_q2, lambda_k2, subln_g, ffn_w_gate, ffn_w_up, ffn_w_down,
           final_norm_g):
    bsz, seq, d = x.shape
    assert meta_tokens.shape == (N_META, d) and N_META == HALO
    rows = _pick_rows(seq, 1024)
    tq = _pick_rows(seq, 512)
    fc = 256
    lambda_init = 0.8 - 0.6 * math.exp(-0.3 * 1)

    row = lambda v: v.reshape(1, -1)
    wg, wu, wd = ([w[layer].astype(BF16) for layer in range(2)] for w in (ffn_w_gate, ffn_w_up, ffn_w_down))
    pw = pool_w[0].astype(BF16)
    wqkv = attn_w_qkv[0].astype(BF16)
    wo = attn_w_o[0].astype(BF16)
    xr = x.reshape(bsz * seq, d)

    l0 = functools.partial(_layer0, g1=row(mix_norm_g[0]), pw=pw, ps=row(pool_scale[0]),
                           g2=row(ffn_norm_g[0]), wg=wg[0], wu=wu[0], wd=wd[0], fc=fc)
    h1 = l0(xr, meta_tokens, rows=rows, tiles_per_seq=seq // rows, pos0=N_META)
    h1_meta = l0(meta_tokens, jnp.zeros_like(meta_tokens), rows=N_META, tiles_per_seq=1, pos0=0)

    qkv = _qkv(h1, row(mix_norm_g[1]), wqkv, rows=rows)
    qkv_meta = _qkv(h1_meta, row(mix_norm_g[1]), wqkv, rows=N_META)

    lam_vecs = jnp.stack([lambda_q1[0], lambda_k1[0], lambda_q2[0], lambda_k2[0]])
    bias_d, bias_m, lam = _bias_tables(rel_bias, lam_vecs, tq=tq, lambda_init=lambda_init)
    attn = _attention(qkv.reshape(bsz, seq, 3 * d), qkv_meta, lam, bias_d, bias_m,
                      row(subln_g[0]), tq=tq, lambda_init=lambda_init, heads=ATTN_HEADS_PER_STEP)

    out = _layer1(attn.reshape(bsz * seq, d), h1, wo, row(ffn_norm_g[1]), wg[1], wu[1], wd[1],
                  row(final_norm_g), rows=rows, fc=fc)
    return out.reshape(bsz, seq, d)
```

```python
import functools
import math

import numpy as np
import jax
import jax.numpy as jnp
from jax import lax
from jax.experimental import pallas as pl
from jax.experimental.pallas import tpu as pltpu

N_META = 16
POOL_WINDOWS = (2, 4, 8, 16)
HEAD_DIM = 64
V_DIM = 2 * HEAD_DIM
N_BUCKETS = 32
MAX_DISTANCE = 128
RMS_EPS = 1e-6
NEG_INF = -1e30
LOG2E = math.log2(math.e)
HALO = 16
BIAS_BLOCK = 128
V7X_VMEM_LIMIT = 56 * 1024 * 1024

F32 = jnp.float32
BF16 = jnp.bfloat16
_NT = (((1,), (1,)), ((), ()))


def _rms(x, g):
    ms = jnp.mean(x * x, axis=-1, keepdims=True)
    return x * lax.rsqrt(ms + RMS_EPS) * g


def _swiglu(hn, wg_ref, wu_ref, wd_ref, fc):
    acc = None
    for c in range(wg_ref.shape[1] // fc):
        sl = slice(c * fc, (c + 1) * fc)
        g = jnp.dot(hn, wg_ref[:, sl], preferred_element_type=F32)
        u = jnp.dot(hn, wu_ref[:, sl], preferred_element_type=F32)
        a = (g * jax.nn.sigmoid(g) * u).astype(BF16)
        d = jnp.dot(a, wd_ref[sl, :], preferred_element_type=F32)
        acc = d if acc is None else acc + d
    return acc


def _layer0_kernel(x_ref, xprev_ref, first_ref, g1_ref, pw_ref, ps_ref, g2_ref,
                   wg_ref, wu_ref, wd_ref, o_ref, *, tiles_per_seq, pos0, fc):
    i = pl.program_id(0)
    rows = x_ref.shape[0]
    gc = pw_ref.shape[1]
    tile_in_seq = i % tiles_per_seq
    x = x_ref[...]
    g1 = g1_ref[...]
    halo_raw = jnp.where(tile_in_seq == 0, first_ref[...], xprev_ref[...])
    hn = _rms(x, g1)
    full = jnp.concatenate([_rms(halo_raw, g1), hn], axis=0)
    pos = lax.broadcasted_iota(jnp.int32, (rows, 1), 0) + tile_in_seq * rows + pos0
    mixes = []
    for g, w in enumerate(POOL_WINDOWS):
        cols = slice(g * gc, (g + 1) * gc)
        s = full[:, cols]
        k = 1
        while k < w:
            s = s + pltpu.roll(s, k, 0)
            k *= 2
        inv_cnt = 1.0 / jnp.minimum(pos + 1, w).astype(F32)
        pooled = s[HALO:] * inv_cnt - hn[:, cols]
        mixes.append(jnp.dot(pooled.astype(BF16), pw_ref[g], preferred_element_type=F32))
    hmid = x + jnp.concatenate(mixes, axis=-1) * ps_ref[...]
    hn2 = _rms(hmid, g2_ref[...]).astype(BF16)
    o_ref[...] = hmid + _swiglu(hn2, wg_ref, wu_ref, wd_ref, fc)


def _resident(shape, layer=None):
    if layer is None:
        return pl.BlockSpec(shape, lambda *_: (0,) * len(shape), pipeline_mode=pl.Buffered(1))
    return pl.BlockSpec((None,) + shape, lambda *_: (layer,) + (0,) * len(shape), pipeline_mode=pl.Buffered(1))


def _layer0(x, first, g1, pw, ps, g2, wg, wu, wd, *, layer, rows, tiles_per_seq, pos0, fc):
    n, d = x.shape
    f = wg.shape[2]
    halo_blocks = rows // HALO
    kern = functools.partial(_layer0_kernel, tiles_per_seq=tiles_per_seq, pos0=pos0, fc=fc)
    return pl.pallas_call(
        kern,
        out_shape=jax.ShapeDtypeStruct((n, d), F32),
        grid=(n // rows,),
        in_specs=[
            pl.BlockSpec((rows, d), lambda i: (i, 0)),
            pl.BlockSpec((HALO, d), lambda i: (jnp.maximum(i * halo_blocks - 1, 0), 0)),
            _resident((HALO, d)),
            _resident((1, d)),
            _resident(pw.shape),
            _resident((1, d)),
            _resident((1, d)),
            _resident((d, f), layer),
            _resident((d, f), layer),
            _resident((f, d), layer),
        ],
        out_specs=pl.BlockSpec((rows, d), lambda i: (i, 0)),
        compiler_params=pltpu.CompilerParams(
            dimension_semantics=("arbitrary",), vmem_limit_bytes=V7X_VMEM_LIMIT),
        name="layer0_pool_swiglu",
    )(x, x, first, g1, pw, ps, g2, wg, wu, wd)


def _qkv_kernel(h_ref, g_ref, w_ref, o_ref):
    d = h_ref.shape[1]
    hn = _rms(h_ref[...], g_ref[...]).astype(BF16)
    for c in range(3):
        sl = slice(c * d, (c + 1) * d)
        y = jnp.dot(hn, w_ref[:, sl], preferred_element_type=F32)
        if c == 0:
            y = y * (LOG2E / math.sqrt(HEAD_DIM))
        o_ref[:, sl] = y.astype(BF16)


def _qkv(h, g, w, *, rows):
    n, d = h.shape
    return pl.pallas_call(
        _qkv_kernel,
        out_shape=jax.ShapeDtypeStruct((n, 3 * d), BF16),
        grid=(n // rows,),
        in_specs=[
            pl.BlockSpec((rows, d), lambda i: (i, 0)),
            _resident((1, d)),
            _resident((d, 3 * d)),
        ],
        out_specs=pl.BlockSpec((rows, 3 * d), lambda i: (i, 0)),
        compiler_params=pltpu.CompilerParams(
            dimension_semantics=("arbitrary",), vmem_limit_bytes=V7X_VMEM_LIMIT),
        name="qkv_proj",
    )(h, g, w)


def _bucket_table(rel):
    n = np.maximum(rel, 0)
    max_exact = N_BUCKETS // 2
    nf = np.maximum(n, max_exact).astype(np.float32)
    large = max_exact + (np.log(nf / np.float32(max_exact)) / np.float32(math.log(MAX_DISTANCE / max_exact))
                         * np.float32(N_BUCKETS - max_exact)).astype(np.int32)
    large = np.minimum(large, N_BUCKETS - 1)
    return np.where(rel < 0, -1, np.where(n < max_exact, n, large)).astype(np.int32)


def _bias_kernel(rb_ref, lv_ref, idxd_ref, idxm_ref, bd_ref, bm_ref, lam_ref, *, lambda_init):
    h = pl.program_id(0)
    far = rb_ref[N_BUCKETS - 1, h]

    def build(idx):
        out = jnp.full(idx.shape, NEG_INF, F32)
        for b in range(N_BUCKETS):
            out = jnp.where(idx == b, (rb_ref[b, h] - far) * LOG2E, out)
        return out

    near = [build(idxd_ref[e]) for e in range(2)]
    const = {True: jnp.full((BIAS_BLOCK, BIAS_BLOCK), NEG_INF, F32),
             False: jnp.zeros((BIAS_BLOCK, BIAS_BLOCK), F32)}
    blocks = bd_ref.shape[1] // BIAS_BLOCK
    for d in range(bd_ref.shape[0]):
        for a in range(blocks):
            for c in range(blocks):
                e = d * blocks + a - c
                bd_ref[d, a * BIAS_BLOCK:(a + 1) * BIAS_BLOCK, c * BIAS_BLOCK:(c + 1) * BIAS_BLOCK] = (
                    near[e] if 0 <= e < 2 else const[e < 0])
    bm_ref[...] = jnp.zeros(bm_ref.shape, F32)
    bm_ref[:BIAS_BLOCK, :] = build(idxm_ref[...])
    lv = lv_ref[...]
    e1 = jnp.exp(jnp.sum(lv[0:1] * lv[1:2], axis=-1, keepdims=True))
    e2 = jnp.exp(jnp.sum(lv[2:3] * lv[3:4], axis=-1, keepdims=True))
    lam_ref[...] = e1 - e2 + lambda_init


def _bias_tables(rel_bias, lam_vecs, *, tq, lambda_init):
    n_heads = rel_bias.shape[1]
    assert tq % BIAS_BLOCK == 0
    qi = np.arange(BIAS_BLOCK)[:, None]
    idx_d = np.stack([_bucket_table(e * BIAS_BLOCK + qi - np.arange(BIAS_BLOCK)[None, :]) for e in range(2)])
    idx_m = _bucket_table(N_META + qi - np.arange(N_META)[None, :])
    kern = functools.partial(_bias_kernel, lambda_init=lambda_init)
    return pl.pallas_call(
        kern,
        out_shape=(jax.ShapeDtypeStruct((n_heads, 2, tq, tq), F32),
                   jax.ShapeDtypeStruct((n_heads, tq, N_META), F32),
                   jax.ShapeDtypeStruct((1, 1), F32)),
        grid=(n_heads,),
        in_specs=[
            pl.BlockSpec(memory_space=pltpu.SMEM),
            pl.BlockSpec(lam_vecs.shape, lambda h: (0, 0)),
            pl.BlockSpec(idx_d.shape, lambda h: (0, 0, 0)),
            pl.BlockSpec(idx_m.shape, lambda h: (0, 0)),
        ],
        out_specs=(pl.BlockSpec((None, 2, tq, tq), lambda h: (h, 0, 0, 0)),
                   pl.BlockSpec((None, tq, N_META), lambda h: (h, 0, 0)),
                   pl.BlockSpec((1, 1), lambda h: (0, 0))),
        compiler_params=pltpu.CompilerParams(dimension_semantics=("arbitrary",)),
        name="rel_bias_tiles",
    )(rel_bias, lam_vecs, jnp.asarray(idx_d), jnp.asarray(idx_m))


def _attn_kernel(lam_ref, q_ref, k_ref, v_ref, km_ref, vm_ref, bd_ref, bm_ref, g_ref,
                 o_ref, vext_ref, qs_ref, s_ref, p_ref, m_ref, acc_ref, *, tq, lambda_init, heads):
    n_q = q_ref.shape[0] // tq
    ones = jnp.ones((1, V_DIM), BF16)
    lanes = [slice(u * V_DIM, (u + 1) * V_DIM) for u in range(heads)]
    lane = lax.broadcasted_iota(jnp.int32, (tq, V_DIM), 1)

    for u in range(heads):
        vext_ref[u, :, :V_DIM] = v_ref[:, lanes[u]]
        vext_ref[u, :, V_DIM:] = jnp.broadcast_to(ones, (v_ref.shape[0], V_DIM))

    def both(b):
        return jnp.concatenate([b, b], axis=0)

    def row_max(s):
        return jnp.broadcast_to(jnp.max(s, axis=-1, keepdims=True), (s.shape[0], V_DIM))

    def wide(r, like):
        return jnp.concatenate([r] * (like.shape[1] // V_DIM), axis=1)

    def scores(u, tile):
        off = pl.multiple_of(tile * tq, tq)
        return lax.dot_general(qs_ref[u], k_ref[pl.ds(off, tq), lanes[u]], _NT, preferred_element_type=F32)

    def weighted_values(u, tile):
        off = pl.multiple_of(tile * tq, tq)
        return jnp.dot(p_ref[u], vext_ref[u, pl.ds(off, tq), :], preferred_element_type=F32)

    def start(i):
        rows = pl.ds(pl.multiple_of(i * tq, tq), tq)
        for u in range(heads):
            q = q_ref[rows, lanes[u]]
            zero = jnp.zeros_like(q)
            qs_ref[u] = jnp.concatenate([jnp.where(lane < HEAD_DIM, q, zero),
                                         jnp.where(lane >= HEAD_DIM, q, zero)], axis=0)
            s_ref[u] = scores(u, i) + both(bd_ref[u, 0])

    def first_step(i, first_tile):
        for u in range(heads):
            s_meta = lax.dot_general(qs_ref[u], km_ref[:, lanes[u]], _NT, preferred_element_type=F32)
            if first_tile:
                s_meta = s_meta + both(bm_ref[u])
            else:
                s_next = scores(u, i - 1) + both(bd_ref[u, 1])
            s = s_ref[u]
            m = jnp.maximum(row_max(s), row_max(s_meta))
            p_meta = jnp.exp2(s_meta - m[:, :N_META]).astype(BF16)
            p_ref[u] = jnp.exp2(s - wide(m, s)).astype(BF16)
            m_ref[u] = m
            if not first_tile:
                s_ref[u] = s_next
            vm_ext = jnp.concatenate([vm_ref[:, lanes[u]], jnp.broadcast_to(ones, (N_META, V_DIM))], axis=-1)
            acc_ref[u] = jnp.dot(p_meta, vm_ext, preferred_element_type=F32)

    def step(next_tile, prev_tile):
        for u in range(heads):
            s_next = scores(u, next_tile)
            pv = weighted_values(u, prev_tile)
            s = s_ref[u]
            m_old = m_ref[u]
            m_new = jnp.maximum(m_old, row_max(s))
            p_ref[u] = jnp.exp2(s - wide(m_new, s)).astype(BF16)
            m_ref[u] = m_new
            s_ref[u] = s_next
            acc_ref[u] = (acc_ref[u] + pv) * wide(jnp.exp2(m_old - m_new), pv)

    def finish(i, last_step):
        rows = pl.ds(pl.multiple_of(i * tq, tq), tq)
        for u in range(heads):
            if last_step:
                pv = weighted_values(u, 1)
                s = s_ref[u]
                m_old = m_ref[u]
                m_new = jnp.maximum(m_old, row_max(s))
                p = jnp.exp2(s - wide(m_new, s)).astype(BF16)
                acc = (acc_ref[u] + pv) * wide(jnp.exp2(m_old - m_new), pv)
                acc = acc + jnp.dot(p, vext_ref[u, :tq, :], preferred_element_type=F32)
            else:
                acc = acc_ref[u] + weighted_values(u, 0)
            o = acc[:, :V_DIM] / acc[:, V_DIM:]
            o = o[:tq] - lam_ref[0, 0] * o[tq:]
            o = _rms(o, g_ref[...]) * (1.0 - lambda_init)
            o_ref[rows, lanes[u]] = o.astype(BF16)

    start(0)

    @pl.loop(0, n_q)
    def _(i):
        for first_tile in (True, False):
            @pl.when((i == 0) == first_tile)
            def _():
                first_step(i, first_tile)

        @pl.loop(1, i)
        def _(t):
            step(i - t - 1, i - t + 1)

        for first_tile, more in ((True, n_q > 1), (False, True), (False, False)):
            @pl.when(((i == 0) == first_tile) & ((i < n_q - 1) == more))
            def _():
                finish(i, last_step=not first_tile)
                if more:
                    start(i + 1)


def _attention(qkv, qkv_meta, lam, bias_d, bias_m, subln_g, *, tq, lambda_init, heads):
    b, s, d3 = qkv.shape
    d = d3 // 3
    n_groups = d // (V_DIM * heads)
    w = V_DIM * heads
    kern = functools.partial(_attn_kernel, tq=tq, lambda_init=lambda_init, heads=heads)
    return pl.pallas_call(
        kern,
        out_shape=jax.ShapeDtypeStruct((b, s, d), BF16),
        grid=(b, n_groups),
        in_specs=[
            pl.BlockSpec(memory_space=pltpu.SMEM),
            pl.BlockSpec((None, s, w), lambda b, g: (b, 0, g)),
            pl.BlockSpec((None, s, w), lambda b, g: (b, 0, n_groups + g)),
            pl.BlockSpec((None, s, w), lambda b, g: (b, 0, 2 * n_groups + g)),
            pl.BlockSpec((N_META, w), lambda b, g: (0, n_groups + g)),
            pl.BlockSpec((N_META, w), lambda b, g: (0, 2 * n_groups + g)),
            pl.BlockSpec((heads, 2, tq, tq), lambda b, g: (g, 0, 0, 0)),
            pl.BlockSpec((heads, tq, N_META), lambda b, g: (g, 0, 0)),
            pl.BlockSpec((1, V_DIM), lambda b, g: (0, 0)),
        ],
        out_specs=pl.BlockSpec((None, s, w), lambda b, g: (b, 0, g)),
        scratch_shapes=[
            pltpu.VMEM((heads, s, 2 * V_DIM), BF16),
            pltpu.VMEM((heads, 2 * tq, V_DIM), BF16),
            pltpu.VMEM((heads, 2 * tq, tq), F32),
            pltpu.VMEM((heads, 2 * tq, tq), BF16),
            pltpu.VMEM((heads, 2 * tq, V_DIM), F32),
            pltpu.VMEM((heads, 2 * tq, 2 * V_DIM), F32),
        ],
        compiler_params=pltpu.CompilerParams(
            dimension_semantics=("arbitrary", "arbitrary"),
            vmem_limit_bytes=V7X_VMEM_LIMIT),
        name="diff_attention",
    )(lam, qkv, qkv, qkv, qkv_meta, qkv_meta, bias_d, bias_m, subln_g)


def _layer1_kernel(a_ref, h_ref, wo_ref, g2_ref, wg_ref, wu_ref, wd_ref, gf_ref, o_ref, *, fc):
    hmid = h_ref[...] + jnp.dot(a_ref[...], wo_ref[...], preferred_element_type=F32)
    hn2 = _rms(hmid, g2_ref[...]).astype(BF16)
    h2 = hmid + _swiglu(hn2, wg_ref, wu_ref, wd_ref, fc)
    o_ref[...] = _rms(h2, gf_ref[...])


def _layer1(a, h, wo, g2, wg, wu, wd, gf, *, layer, rows, fc):
    n, d = h.shape
    f = wg.shape[2]
    kern = functools.partial(_layer1_kernel, fc=fc)
    return pl.pallas_call(
        kern,
        out_shape=jax.ShapeDtypeStruct((n, d), F32),
        grid=(n // rows,),
        in_specs=[
            pl.BlockSpec((rows, d), lambda i: (i, 0)),
            pl.BlockSpec((rows, d), lambda i: (i, 0)),
            _resident((d, d)),
            _resident((1, d)),
            _resident((d, f), layer),
            _resident((d, f), layer),
            _resident((f, d), layer),
            _resident((1, d)),
        ],
        out_specs=pl.BlockSpec((rows, d), lambda i: (i, 0)),
        compiler_params=pltpu.CompilerParams(
            dimension_semantics=("arbitrary",), vmem_limit_bytes=V7X_VMEM_LIMIT),
        name="layer1_out_swiglu_norm",
    )(a, h, wo, g2, wg, wu, wd, gf)


ATTN_HEADS_PER_STEP = 2


def _pick_rows(seq, want):
    rows = min(seq, want)
    assert seq % rows == 0 and rows % HALO == 0
    return rows


def kernel(x, meta_tokens, rel_bias, mix_norm_g, ffn_norm_g, pool_w, pool_scale, attn_w_qkv, attn_w_o,
           lambda_q1, lambda_k1, lambda_q2, lambda_k2, subln_g, ffn_w_gate, ffn_w_up, ffn_w_down,
           final_norm_g):
    bsz, seq, d = x.shape
    assert meta_tokens.shape == (N_META, d) and N_META == HALO
    rows = _pick_rows(seq, 1024)
    tq = _pick_rows(seq, 512)
    fc = 256
    lambda_init = 0.8 - 0.6 * math.exp(-0.3 * 1)

    row = lambda v: v.reshape(1, -1)
    wg, wu, wd = (w.astype(BF16) for w in (ffn_w_gate, ffn_w_up, ffn_w_down))
    pw = pool_w[0].astype(BF16)
    wqkv = attn_w_qkv[0].astype(BF16)
    wo = attn_w_o[0].astype(BF16)
    xr = x.reshape(bsz * seq, d)

    l0 = functools.partial(_layer0, g1=row(mix_norm_g[0]), pw=pw, ps=row(pool_scale[0]),
                           g2=row(ffn_norm_g[0]), wg=wg, wu=wu, wd=wd, layer=0, fc=fc)
    h1 = l0(xr, meta_tokens, rows=rows, tiles_per_seq=seq // rows, pos0=N_META)
    h1_meta = l0(meta_tokens, jnp.zeros_like(meta_tokens), rows=N_META, tiles_per_seq=1, pos0=0)

    qkv = _qkv(h1, row(mix_norm_g[1]), wqkv, rows=rows)
    qkv_meta = _qkv(h1_meta, row(mix_norm_g[1]), wqkv, rows=N_META)

    lam_vecs = jnp.stack([lambda_q1[0], lambda_k1[0], lambda_q2[0], lambda_k2[0]])
    bias_d, bias_m, lam = _bias_tables(rel_bias, lam_vecs, tq=tq, lambda_init=lambda_init)
    attn = _attention(qkv.reshape(bsz, seq, 3 * d), qkv_meta, lam, bias_d, bias_m,
                      row(subln_g[0]), tq=tq, lambda_init=lambda_init, heads=ATTN_HEADS_PER_STEP)

    out = _layer1(attn.reshape(bsz * seq, d), h1, wo, row(ffn_norm_g[1]), wg, wu, wd,
                  row(final_norm_g), layer=1, rows=rows, fc=fc)
    return out.reshape(bsz, seq, d)
```

```python
import functools
import math

import numpy as np
import jax
import jax.numpy as jnp
from jax import lax
from jax.experimental import pallas as pl
from jax.experimental.pallas import tpu as pltpu

N_META = 16
POOL_WINDOWS = (2, 4, 8, 16)
HEAD_DIM = 64
V_DIM = 2 * HEAD_DIM
N_BUCKETS = 32
MAX_DISTANCE = 128
RMS_EPS = 1e-6
NEG_INF = -1e30
LOG2E = math.log2(math.e)
HALO = 16
SUB_ROWS = 512
BIAS_BLOCK = 128
V7X_VMEM_LIMIT = 56 * 1024 * 1024

F32 = jnp.float32
BF16 = jnp.bfloat16
_NT = (((1,), (1,)), ((), ()))


def _rms(x, g):
    ms = jnp.mean(x * x, axis=-1, keepdims=True)
    return x * lax.rsqrt(ms + RMS_EPS) * g


def _swiglu_steps(hn, wg_ref, wu_ref, wd_ref, fc):
    acc = None
    for c in range(wg_ref.shape[1] // fc):
        sl = slice(c * fc, (c + 1) * fc)
        g = jnp.dot(hn, wg_ref[:, sl], preferred_element_type=F32)
        u = jnp.dot(hn, wu_ref[:, sl], preferred_element_type=F32)
        a = (g * jax.nn.sigmoid(g) * u).astype(BF16)
        d = jnp.dot(a, wd_ref[sl, :], preferred_element_type=F32)
        acc = d if acc is None else acc + d
        yield acc


def _interleave(steps, stages):
    stages = list(stages)
    done = 0
    for k, value in enumerate(steps):
        while stages and done * 2 <= k:
            stages.pop(0)()
            done += 1
    for stage in stages:
        stage()
    return value


def _swiglu(hn, wg_ref, wu_ref, wd_ref, fc):
    return _interleave(_swiglu_steps(hn, wg_ref, wu_ref, wd_ref, fc), [])


def _layer0_kernel(x_ref, xprev_ref, first_ref, g1_ref, pw_ref, ps_ref, g2_ref,
                   wg_ref, wu_ref, wd_ref, o_ref, *, tiles_per_seq, pos0, fc, sub):
    i = pl.program_id(0)
    rows = x_ref.shape[0]
    gc = pw_ref.shape[1]
    tile_in_seq = i % tiles_per_seq
    g1 = g1_ref[...]

    def mixer_stages(c, halo):
        r = slice(c * sub, (c + 1) * sub)
        st = {"rows": r, "mix": []}

        def norm():
            st["x"] = x_ref[r, :]
            st["hn"] = _rms(st["x"], g1)
            st["full"] = jnp.concatenate([halo(), st["hn"]], axis=0)
            st["pos"] = lax.broadcasted_iota(jnp.int32, (sub, 1), 0) + tile_in_seq * rows + c * sub + pos0

        def pool(g, w):
            cols = slice(g * gc, (g + 1) * gc)
            s = st["full"][:, cols]
            k = 1
            while k < w:
                s = s + pltpu.roll(s, k, 0)
                k *= 2
            inv_cnt = 1.0 / jnp.minimum(st["pos"] + 1, w).astype(F32)
            pooled = s[HALO:] * inv_cnt - st["hn"][:, cols]
            st["mix"].append(jnp.dot(pooled.astype(BF16), pw_ref[g], preferred_element_type=F32))

        def residual():
            st["hmid"] = st["x"] + jnp.concatenate(st["mix"], axis=-1) * ps_ref[...]
            st["hn2"] = _rms(st["hmid"], g2_ref[...]).astype(BF16)

        stages = [norm] + [functools.partial(pool, g, w) for g, w in enumerate(POOL_WINDOWS)] + [residual]
        return stages, st

    first_halo = lambda: _rms(jnp.where(tile_in_seq == 0, first_ref[...], xprev_ref[...]), g1)
    stages, st = mixer_stages(0, first_halo)
    for stage in stages:
        stage()
    for c in range(rows // sub):
        nxt_stages, nxt = [], None
        if c + 1 < rows // sub:
            nxt_stages, nxt = mixer_stages(c + 1, lambda prev=st: prev["hn"][sub - HALO:])
        ffn = _interleave(_swiglu_steps(st["hn2"], wg_ref, wu_ref, wd_ref, fc), nxt_stages)
        o_ref[st["rows"], :] = st["hmid"] + ffn
        st = nxt


def _resident(shape, layer=None):
    if layer is None:
        return pl.BlockSpec(shape, lambda *_: (0,) * len(shape), pipeline_mode=pl.Buffered(1))
    return pl.BlockSpec((None,) + shape, lambda *_: (layer,) + (0,) * len(shape), pipeline_mode=pl.Buffered(1))


def _layer0(x, first, g1, pw, ps, g2, wg, wu, wd, *, layer, rows, tiles_per_seq, pos0, fc):
    n, d = x.shape
    f = wg.shape[2]
    halo_blocks = rows // HALO
    kern = functools.partial(_layer0_kernel, tiles_per_seq=tiles_per_seq, pos0=pos0, fc=fc,
                             sub=min(rows, SUB_ROWS))
    return pl.pallas_call(
        kern,
        out_shape=jax.ShapeDtypeStruct((n, d), F32),
        grid=(n // rows,),
        in_specs=[
            pl.BlockSpec((rows, d), lambda i: (i, 0)),
            pl.BlockSpec((HALO, d), lambda i: (jnp.maximum(i * halo_blocks - 1, 0), 0)),
            _resident((HALO, d)),
            _resident((1, d)),
            _resident(pw.shape),
            _resident((1, d)),
            _resident((1, d)),
            _resident((d, f), layer),
            _resident((d, f), layer),
            _resident((f, d), layer),
        ],
        out_specs=pl.BlockSpec((rows, d), lambda i: (i, 0)),
        compiler_params=pltpu.CompilerParams(
            dimension_semantics=("arbitrary",), vmem_limit_bytes=V7X_VMEM_LIMIT),
        name="layer0_pool_swiglu",
    )(x, x, first, g1, pw, ps, g2, wg, wu, wd)


def _qkv_kernel(h_ref, g_ref, w_ref, o_ref):
    d = h_ref.shape[1]
    hn = _rms(h_ref[...], g_ref[...]).astype(BF16)
    for c in range(3):
        sl = slice(c * d, (c + 1) * d)
        y = jnp.dot(hn, w_ref[:, sl], preferred_element_type=F32)
        if c == 0:
            y = y * (LOG2E / math.sqrt(HEAD_DIM))
        o_ref[:, sl] = y.astype(BF16)


def _qkv(h, g, w, *, rows):
    n, d = h.shape
    return pl.pallas_call(
        _qkv_kernel,
        out_shape=jax.ShapeDtypeStruct((n, 3 * d), BF16),
        grid=(n // rows,),
        in_specs=[
            pl.BlockSpec((rows, d), lambda i: (i, 0)),
            _resident((1, d)),
            _resident((d, 3 * d)),
        ],
        out_specs=pl.BlockSpec((rows, 3 * d), lambda i: (i, 0)),
        compiler_params=pltpu.CompilerParams(
            dimension_semantics=("arbitrary",), vmem_limit_bytes=V7X_VMEM_LIMIT),
        name="qkv_proj",
    )(h, g, w)


def _bucket_table(rel):
    n = np.maximum(rel, 0)
    max_exact = N_BUCKETS // 2
    nf = np.maximum(n, max_exact).astype(np.float32)
    large = max_exact + (np.log(nf / np.float32(max_exact)) / np.float32(math.log(MAX_DISTANCE / max_exact))
                         * np.float32(N_BUCKETS - max_exact)).astype(np.int32)
    large = np.minimum(large, N_BUCKETS - 1)
    return np.where(rel < 0, -1, np.where(n < max_exact, n, large)).astype(np.int32)


def _bias_kernel(rb_ref, lv_ref, idxd_ref, idxm_ref, bd_ref, bm_ref, lam_ref, *, lambda_init):
    h = pl.program_id(0)
    far = rb_ref[N_BUCKETS - 1, h]

    def build(idx):
        out = jnp.full(idx.shape, NEG_INF, F32)
        for b in range(N_BUCKETS):
            out = jnp.where(idx == b, (rb_ref[b, h] - far) * LOG2E, out)
        return out

    near = [build(idxd_ref[e]) for e in range(2)]
    const = {True: jnp.full((BIAS_BLOCK, BIAS_BLOCK), NEG_INF, F32),
             False: jnp.zeros((BIAS_BLOCK, BIAS_BLOCK), F32)}
    blocks = bd_ref.shape[1] // BIAS_BLOCK
    for d in range(bd_ref.shape[0]):
        for a in range(blocks):
            for c in range(blocks):
                e = d * blocks + a - c
                bd_ref[d, a * BIAS_BLOCK:(a + 1) * BIAS_BLOCK, c * BIAS_BLOCK:(c + 1) * BIAS_BLOCK] = (
                    near[e] if 0 <= e < 2 else const[e < 0])
    bm_ref[...] = jnp.zeros(bm_ref.shape, F32)
    bm_ref[:BIAS_BLOCK, :] = build(idxm_ref[...])
    lv = lv_ref[...]
    e1 = jnp.exp(jnp.sum(lv[0:1] * lv[1:2], axis=-1, keepdims=True))
    e2 = jnp.exp(jnp.sum(lv[2:3] * lv[3:4], axis=-1, keepdims=True))
    lam_ref[...] = e1 - e2 + lambda_init


def _bias_tables(rel_bias, lam_vecs, *, tq, lambda_init):
    n_heads = rel_bias.shape[1]
    assert tq % BIAS_BLOCK == 0
    qi = np.arange(BIAS_BLOCK)[:, None]
    idx_d = np.stack([_bucket_table(e * BIAS_BLOCK + qi - np.arange(BIAS_BLOCK)[None, :]) for e in range(2)])
    idx_m = _bucket_table(N_META + qi - np.arange(N_META)[None, :])
    kern = functools.partial(_bias_kernel, lambda_init=lambda_init)
    return pl.pallas_call(
        kern,
        out_shape=(jax.ShapeDtypeStruct((n_heads, 2, tq, tq), F32),
                   jax.ShapeDtypeStruct((n_heads, tq, N_META), F32),
                   jax.ShapeDtypeStruct((1, 1), F32)),
        grid=(n_heads,),
        in_specs=[
            pl.BlockSpec(memory_space=pltpu.SMEM),
            pl.BlockSpec(lam_vecs.shape, lambda h: (0, 0)),
            pl.BlockSpec(idx_d.shape, lambda h: (0, 0, 0)),
            pl.BlockSpec(idx_m.shape, lambda h: (0, 0)),
        ],
        out_specs=(pl.BlockSpec((None, 2, tq, tq), lambda h: (h, 0, 0, 0)),
                   pl.BlockSpec((None, tq, N_META), lambda h: (h, 0, 0)),
                   pl.BlockSpec((1, 1), lambda h: (0, 0))),
        compiler_params=pltpu.CompilerParams(dimension_semantics=("arbitrary",)),
        name="rel_bias_tiles",
    )(rel_bias, lam_vecs, jnp.asarray(idx_d), jnp.asarray(idx_m))


def _attn_kernel(lam_ref, q_ref, k_ref, v_ref, km_ref, vm_ref, bd_ref, bm_ref, g_ref,
                 o_ref, vext_ref, qs_ref, s_ref, p_ref, m_ref, acc_ref, *, tq, lambda_init, heads):
    n_q = q_ref.shape[0] // tq
    ones = jnp.ones((1, V_DIM), BF16)
    lanes = [slice(u * V_DIM, (u + 1) * V_DIM) for u in range(heads)]
    lane = lax.broadcasted_iota(jnp.int32, (tq, V_DIM), 1)

    for u in range(heads):
        vext_ref[u, :, :V_DIM] = v_ref[:, lanes[u]]
        vext_ref[u, :, V_DIM:] = jnp.broadcast_to(ones, (v_ref.shape[0], V_DIM))

    def both(b):
        return jnp.concatenate([b, b], axis=0)

    def row_max(s):
        return jnp.broadcast_to(jnp.max(s, axis=-1, keepdims=True), (s.shape[0], V_DIM))

    def wide(r, like):
        return jnp.concatenate([r] * (like.shape[1] // V_DIM), axis=1)

    def scores(u, tile):
        off = pl.multiple_of(tile * tq, tq)
        return lax.dot_general(qs_ref[u], k_ref[pl.ds(off, tq), lanes[u]], _NT, preferred_element_type=F32)

    def weighted_values(u, tile):
        off = pl.multiple_of(tile * tq, tq)
        return jnp.dot(p_ref[u], vext_ref[u, pl.ds(off, tq), :], preferred_element_type=F32)

    def start(i):
        rows = pl.ds(pl.multiple_of(i * tq, tq), tq)
        for u in range(heads):
            q = q_ref[rows, lanes[u]]
            zero = jnp.zeros_like(q)
            qs_ref[u] = jnp.concatenate([jnp.where(lane < HEAD_DIM, q, zero),
                                         jnp.where(lane >= HEAD_DIM, q, zero)], axis=0)
            s_ref[u] = scores(u, i) + both(bd_ref[u, 0])

    def first_step(i, first_tile):
        for u in range(heads):
            s_meta = lax.dot_general(qs_ref[u], km_ref[:, lanes[u]], _NT, preferred_element_type=F32)
            if first_tile:
                s_meta = s_meta + both(bm_ref[u])
            else:
                s_next = scores(u, i - 1) + both(bd_ref[u, 1])
            s = s_ref[u]
            m = jnp.maximum(row_max(s), row_max(s_meta))
            p_meta = jnp.exp2(s_meta - m[:, :N_META]).astype(BF16)
            p_ref[u] = jnp.exp2(s - wide(m, s)).astype(BF16)
            m_ref[u] = m
            if not first_tile:
                s_ref[u] = s_next
            vm_ext = jnp.concatenate([vm_ref[:, lanes[u]], jnp.broadcast_to(ones, (N_META, V_DIM))], axis=-1)
            acc_ref[u] = jnp.dot(p_meta, vm_ext, preferred_element_type=F32)

    def step(next_tile, prev_tile):
        for u in range(heads):
            s_next = scores(u, next_tile)
            pv = weighted_values(u, prev_tile)
            s = s_ref[u]
            m_old = m_ref[u]
            m_new = jnp.maximum(m_old, row_max(s))
            p_ref[u] = jnp.exp2(s - wide(m_new, s)).astype(BF16)
            m_ref[u] = m_new
            s_ref[u] = s_next
            acc_ref[u] = (acc_ref[u] + pv) * wide(jnp.exp2(m_old - m_new), pv)

    def finish(i, last_step):
        rows = pl.ds(pl.multiple_of(i * tq, tq), tq)
        for u in range(heads):
            if last_step:
                pv = weighted_values(u, 1)
                s = s_ref[u]
                m_old = m_ref[u]
                m_new = jnp.maximum(m_old, row_max(s))
                p = jnp.exp2(s - wide(m_new, s)).astype(BF16)
                acc = (acc_ref[u] + pv) * wide(jnp.exp2(m_old - m_new), pv)
                acc = acc + jnp.dot(p, vext_ref[u, :tq, :], preferred_element_type=F32)
            else:
                acc = acc_ref[u] + weighted_values(u, 0)
            o = acc[:, :V_DIM] / acc[:, V_DIM:]
            o = o[:tq] - lam_ref[0, 0] * o[tq:]
            o = _rms(o, g_ref[...]) * (1.0 - lambda_init)
            o_ref[rows, lanes[u]] = o.astype(BF16)

    start(0)

    @pl.loop(0, n_q)
    def _(i):
        for first_tile in (True, False):
            @pl.when((i == 0) == first_tile)
            def _():
                first_step(i, first_tile)

        @pl.loop(1, i)
        def _(t):
            step(i - t - 1, i - t + 1)

        for first_tile, more in ((True, n_q > 1), (False, True), (False, False)):
            @pl.when(((i == 0) == first_tile) & ((i < n_q - 1) == more))
            def _():
                finish(i, last_step=not first_tile)
                if more:
                    start(i + 1)


def _attention(qkv, qkv_meta, lam, bias_d, bias_m, subln_g, *, tq, lambda_init, heads):
    b, s, d3 = qkv.shape
    d = d3 // 3
    n_groups = d // (V_DIM * heads)
    w = V_DIM * heads
    kern = functools.partial(_attn_kernel, tq=tq, lambda_init=lambda_init, heads=heads)
    return pl.pallas_call(
        kern,
        out_shape=jax.ShapeDtypeStruct((b, s, d), BF16),
        grid=(b, n_groups),
        in_specs=[
            pl.BlockSpec(memory_space=pltpu.SMEM),
            pl.BlockSpec((None, s, w), lambda b, g: (b, 0, g)),
            pl.BlockSpec((None, s, w), lambda b, g: (b, 0, n_groups + g)),
            pl.BlockSpec((None, s, w), lambda b, g: (b, 0, 2 * n_groups + g)),
            pl.BlockSpec((N_META, w), lambda b, g: (0, n_groups + g)),
            pl.BlockSpec((N_META, w), lambda b, g: (0, 2 * n_groups + g)),
            pl.BlockSpec((heads, 2, tq, tq), lambda b, g: (g, 0, 0, 0)),
            pl.BlockSpec((heads, tq, N_META), lambda b, g: (g, 0, 0)),
            pl.BlockSpec((1, V_DIM), lambda b, g: (0, 0)),
        ],
        out_specs=pl.BlockSpec((None, s, w), lambda b, g: (b, 0, g)),
        scratch_shapes=[
            pltpu.VMEM((heads, s, 2 * V_DIM), BF16),
            pltpu.VMEM((heads, 2 * tq, V_DIM), BF16),
            pltpu.VMEM((heads, 2 * tq, tq), F32),
            pltpu.VMEM((heads, 2 * tq, tq), BF16),
            pltpu.VMEM((heads, 2 * tq, V_DIM), F32),
            pltpu.VMEM((heads, 2 * tq, 2 * V_DIM), F32),
        ],
        compiler_params=pltpu.CompilerParams(
            dimension_semantics=("arbitrary", "arbitrary"),
            vmem_limit_bytes=V7X_VMEM_LIMIT),
        name="diff_attention",
    )(lam, qkv, qkv, qkv, qkv_meta, qkv_meta, bias_d, bias_m, subln_g)


def _layer1_kernel(a_ref, h_ref, wo_ref, g2_ref, wg_ref, wu_ref, wd_ref, gf_ref, o_ref, *, fc):
    sub = min(a_ref.shape[0], SUB_ROWS)
    n_sub = a_ref.shape[0] // sub

    def prepare(c):
        r = slice(c * sub, (c + 1) * sub)
        st = {"rows": r}

        def out_proj():
            st["hmid"] = h_ref[r, :] + jnp.dot(a_ref[r, :], wo_ref[...], preferred_element_type=F32)
            st["hn2"] = _rms(st["hmid"], g2_ref[...]).astype(BF16)

        return [out_proj], st

    def finalize(st, ffn):
        o_ref[st["rows"], :] = _rms(st["hmid"] + ffn, gf_ref[...])

    stages, st = prepare(0)
    stages[0]()
    pending = []
    for c in range(n_sub):
        nxt_stages, nxt = prepare(c + 1) if c + 1 < n_sub else ([], None)
        ffn = _interleave(_swiglu_steps(st["hn2"], wg_ref, wu_ref, wd_ref, fc), pending + nxt_stages)
        pending = [functools.partial(finalize, st, ffn)]
        st = nxt
    pending[0]()


def _layer1(a, h, wo, g2, wg, wu, wd, gf, *, layer, rows, fc):
    n, d = h.shape
    f = wg.shape[2]
    kern = functools.partial(_layer1_kernel, fc=fc)
    return pl.pallas_call(
        kern,
        out_shape=jax.ShapeDtypeStruct((n, d), F32),
        grid=(n // rows,),
        in_specs=[
            pl.BlockSpec((rows, d), lambda i: (i, 0)),
            pl.BlockSpec((rows, d), lambda i: (i, 0)),
            _resident((d, d)),
            _resident((1, d)),
            _resident((d, f), layer),
            _resident((d, f), layer),
            _resident((f, d), layer),
            _resident((1, d)),
        ],
        out_specs=pl.BlockSpec((rows, d), lambda i: (i, 0)),
        compiler_params=pltpu.CompilerParams(
            dimension_semantics=("arbitrary",), vmem_limit_bytes=V7X_VMEM_LIMIT),
        name="layer1_out_swiglu_norm",
    )(a, h, wo, g2, wg, wu, wd, gf)


ATTN_HEADS_PER_STEP = 2


def _pick_rows(seq, want):
    rows = min(seq, want)
    assert seq % rows == 0 and rows % HALO == 0
    return rows


def kernel(x, meta_tokens, rel_bias, mix_norm_g, ffn_norm_g, pool_w, pool_scale, attn_w_qkv, attn_w_o,
           lambda_q1, lambda_k1, lambda_q2, lambda_k2, subln_g, ffn_w_gate, ffn_w_up, ffn_w_down,
           final_norm_g):
    bsz, seq, d = x.shape
    assert meta_tokens.shape == (N_META, d) and N_META == HALO
    rows = _pick_rows(seq, 1024)
    tq = _pick_rows(seq, 512)
    fc = 256
    lambda_init = 0.8 - 0.6 * math.exp(-0.3 * 1)

    row = lambda v: v.reshape(1, -1)
    wg, wu, wd = (w.astype(BF16) for w in (ffn_w_gate, ffn_w_up, ffn_w_down))
    pw = pool_w[0].astype(BF16)
    wqkv = attn_w_qkv[0].astype(BF16)
    wo = attn_w_o[0].astype(BF16)
    xr = x.reshape(bsz * seq, d)

    l0 = functools.partial(_layer0, g1=row(mix_norm_g[0]), pw=pw, ps=row(pool_scale[0]),
                           g2=row(ffn_norm_g[0]), wg=wg, wu=wu, wd=wd, layer=0, fc=fc)
    h1 = l0(xr, meta_tokens, rows=rows, tiles_per_seq=seq // rows, pos0=N_META)
    h1_meta = l0(meta_tokens, jnp.zeros_like(meta_tokens), rows=N_META, tiles_per_seq=1, pos0=0)

    qkv = _qkv(h1, row(mix_norm_g[1]), wqkv, rows=rows)
    qkv_meta = _qkv(h1_meta, row(mix_norm_g[1]), wqkv, rows=N_META)

    lam_vecs = jnp.stack([lambda_q1[0], lambda_k1[0], lambda_q2[0], lambda_k2[0]])
    bias_d, bias_m, lam = _bias_tables(rel_bias, lam_vecs, tq=tq, lambda_init=lambda_init)
    attn = _attention(qkv.reshape(bsz, seq, 3 * d), qkv_meta, lam, bias_d, bias_m,
                      row(subln_g[0]), tq=tq, lambda_init=lambda_init, heads=ATTN_HEADS_PER_STEP)

    out = _layer1(attn.reshape(bsz * seq, d), h1, wo, row(ffn_norm_g[1]), wg, wu, wd,
                  row(final_norm_g), layer=1, rows=rows, fc=fc)
    return out.reshape(bsz, seq, d)
```

```python
import functools
import math

import numpy as np
import jax
import jax.numpy as jnp
from jax import lax
from jax.experimental import pallas as pl
from jax.experimental.pallas import tpu as pltpu

N_META = 16
POOL_WINDOWS = (2, 4, 8, 16)
HEAD_DIM = 64
V_DIM = 2 * HEAD_DIM
N_BUCKETS = 32
MAX_DISTANCE = 128
RMS_EPS = 1e-6
NEG_INF = -1e30
LOG2E = math.log2(math.e)
HALO = 16
SUB_ROWS = 512
BIAS_BLOCK = 128
V7X_VMEM_LIMIT = 56 * 1024 * 1024

F32 = jnp.float32
BF16 = jnp.bfloat16
_NT = (((1,), (1,)), ((), ()))


def _rms(x, g):
    ms = jnp.mean(x * x, axis=-1, keepdims=True)
    return x * lax.rsqrt(ms + RMS_EPS) * g


def _swiglu_steps(hn, wg_ref, wu_ref, wd_ref, fc):
    acc = None
    for c in range(wg_ref.shape[1] // fc):
        sl = slice(c * fc, (c + 1) * fc)
        g = jnp.dot(hn, wg_ref[:, sl], preferred_element_type=F32)
        u = jnp.dot(hn, wu_ref[:, sl], preferred_element_type=F32)
        a = (g * jax.nn.sigmoid(g) * u).astype(BF16)
        d = jnp.dot(a, wd_ref[sl, :], preferred_element_type=F32)
        acc = d if acc is None else acc + d
        yield acc


def _interleave(steps, stages):
    stages = list(stages)
    done = 0
    for k, value in enumerate(steps):
        while stages and done * 2 <= k:
            stages.pop(0)()
            done += 1
    for stage in stages:
        stage()
    return value


def _swiglu(hn, wg_ref, wu_ref, wd_ref, fc):
    return _interleave(_swiglu_steps(hn, wg_ref, wu_ref, wd_ref, fc), [])


def _layer0_kernel(x_ref, xnext_ref, first_ref, g1_ref, pw_ref, ps_ref, g2_ref,
                   wg_ref, wu_ref, wd_ref, o_ref, hmid_ref, hn2_ref, *, tiles_per_seq, pos0, fc, sub):
    i = pl.program_id(0)
    rows = x_ref.shape[0]
    n_sub = rows // sub
    gc = pw_ref.shape[1]
    g1 = g1_ref[...]
    slot = i % 2

    def mixer_stages(load_x, halo, pos_base):
        st = {"mix": []}

        def norm():
            st["x"] = load_x()
            st["hn"] = _rms(st["x"], g1)
            st["full"] = jnp.concatenate([_rms(halo(), g1), st["hn"]], axis=0)
            st["pos"] = lax.broadcasted_iota(jnp.int32, (sub, 1), 0) + pos_base

        def pool(g, w):
            cols = slice(g * gc, (g + 1) * gc)
            s = st["full"][:, cols]
            k = 1
            while k < w:
                s = s + pltpu.roll(s, k, 0)
                k *= 2
            inv_cnt = 1.0 / jnp.minimum(st["pos"] + 1, w).astype(F32)
            pooled = s[HALO:] * inv_cnt - st["hn"][:, cols]
            st["mix"].append(jnp.dot(pooled.astype(BF16), pw_ref[g], preferred_element_type=F32))

        def residual():
            st["hmid"] = st["x"] + jnp.concatenate(st["mix"], axis=-1) * ps_ref[...]
            st["hn2"] = _rms(st["hmid"], g2_ref[...]).astype(BF16)

        stages = [norm] + [functools.partial(pool, g, w) for g, w in enumerate(POOL_WINDOWS)] + [residual]
        return stages, st

    @pl.when(i == 0)
    def _():
        stages, st = mixer_stages(lambda: x_ref[:sub, :], lambda: first_ref[...], pos0)
        for stage in stages:
            stage()
        hmid_ref[0] = st["hmid"]
        hn2_ref[0] = st["hn2"]

    tile_in_seq = i % tiles_per_seq
    next_in_seq = (i + 1) % tiles_per_seq
    st = None
    for c in range(n_sub):
        r = slice(c * sub, (c + 1) * sub)
        if c + 1 < n_sub:
            nr = slice((c + 1) * sub, (c + 2) * sub)
            nxt_stages, nxt = mixer_stages(lambda nr=nr: x_ref[nr, :],
                                           lambda nr=nr: x_ref[nr.start - HALO:nr.start, :],
                                           tile_in_seq * rows + nr.start + pos0)
        else:
            nxt_stages, nxt = mixer_stages(
                lambda: xnext_ref[...],
                lambda: jnp.where(next_in_seq == 0, first_ref[...], x_ref[rows - HALO:, :]),
                next_in_seq * rows + pos0)

            def hand_over(nxt=nxt):
                hmid_ref[1 - slot] = nxt["hmid"]
                hn2_ref[1 - slot] = nxt["hn2"]

            nxt_stages = nxt_stages + [hand_over]
        hn2 = hn2_ref[slot] if c == 0 else st["hn2"]
        ffn = _interleave(_swiglu_steps(hn2, wg_ref, wu_ref, wd_ref, fc), nxt_stages)
        o_ref[r, :] = (hmid_ref[slot] if c == 0 else st["hmid"]) + ffn
        st = nxt


def _resident(shape, layer=None):
    if layer is None:
        return pl.BlockSpec(shape, lambda *_: (0,) * len(shape), pipeline_mode=pl.Buffered(1))
    return pl.BlockSpec((None,) + shape, lambda *_: (layer,) + (0,) * len(shape), pipeline_mode=pl.Buffered(1))


def _layer0(x, first, g1, pw, ps, g2, wg, wu, wd, *, layer, rows, tiles_per_seq, pos0, fc):
    n, d = x.shape
    f = wg.shape[2]
    sub = min(rows, SUB_ROWS)
    n_sub = rows // sub
    kern = functools.partial(_layer0_kernel, tiles_per_seq=tiles_per_seq, pos0=pos0, fc=fc, sub=sub)
    return pl.pallas_call(
        kern,
        out_shape=jax.ShapeDtypeStruct((n, d), F32),
        grid=(n // rows,),
        in_specs=[
            pl.BlockSpec((rows, d), lambda i: (i, 0)),
            pl.BlockSpec((sub, d), lambda i: (jnp.minimum((i + 1) * n_sub, n // sub - 1), 0)),
            _resident((HALO, d)),
            _resident((1, d)),
            _resident(pw.shape),
            _resident((1, d)),
            _resident((1, d)),
            _resident((d, f), layer),
            _resident((d, f), layer),
            _resident((f, d), layer),
        ],
        out_specs=pl.BlockSpec((rows, d), lambda i: (i, 0)),
        scratch_shapes=[pltpu.VMEM((2, sub, d), F32),
                        pltpu.VMEM((2, sub, d), BF16)],
        compiler_params=pltpu.CompilerParams(
            dimension_semantics=("arbitrary",), vmem_limit_bytes=V7X_VMEM_LIMIT),
        name="layer0_pool_swiglu",
    )(x, x, first, g1, pw, ps, g2, wg, wu, wd)


def _qkv_kernel(h_ref, g_ref, w_ref, o_ref):
    d = h_ref.shape[1]
    hn = _rms(h_ref[...], g_ref[...]).astype(BF16)
    for c in range(3):
        sl = slice(c * d, (c + 1) * d)
        y = jnp.dot(hn, w_ref[:, sl], preferred_element_type=F32)
        if c == 0:
            y = y * (LOG2E / math.sqrt(HEAD_DIM))
        o_ref[:, sl] = y.astype(BF16)


def _qkv(h, g, w, *, rows):
    n, d = h.shape
    return pl.pallas_call(
        _qkv_kernel,
        out_shape=jax.ShapeDtypeStruct((n, 3 * d), BF16),
        grid=(n // rows,),
        in_specs=[
            pl.BlockSpec((rows, d), lambda i: (i, 0)),
            _resident((1, d)),
            _resident((d, 3 * d)),
        ],
        out_specs=pl.BlockSpec((rows, 3 * d), lambda i: (i, 0)),
        compiler_params=pltpu.CompilerParams(
            dimension_semantics=("arbitrary",), vmem_limit_bytes=V7X_VMEM_LIMIT),
        name="qkv_proj",
    )(h, g, w)


def _bucket_table(rel):
    n = np.maximum(rel, 0)
    max_exact = N_BUCKETS // 2
    nf = np.maximum(n, max_exact).astype(np.float32)
    large = max_exact + (np.log(nf / np.float32(max_exact)) / np.float32(math.log(MAX_DISTANCE / max_exact))
                         * np.float32(N_BUCKETS - max_exact)).astype(np.int32)
    large = np.minimum(large, N_BUCKETS - 1)
    return np.where(rel < 0, -1, np.where(n < max_exact, n, large)).astype(np.int32)


def _bias_kernel(rb_ref, lv_ref, idxd_ref, idxm_ref, bd_ref, bm_ref, lam_ref, *, lambda_init):
    h = pl.program_id(0)
    far = rb_ref[N_BUCKETS - 1, h]

    def build(idx):
        out = jnp.full(idx.shape, NEG_INF, F32)
        for b in range(N_BUCKETS):
            out = jnp.where(idx == b, (rb_ref[b, h] - far) * LOG2E, out)
        return out

    near = [build(idxd_ref[e]) for e in range(2)]
    const = {True: jnp.full((BIAS_BLOCK, BIAS_BLOCK), NEG_INF, F32),
             False: jnp.zeros((BIAS_BLOCK, BIAS_BLOCK), F32)}
    blocks = bd_ref.shape[1] // BIAS_BLOCK
    for d in range(bd_ref.shape[0]):
        for a in range(blocks):
            for c in range(blocks):
                e = d * blocks + a - c
                bd_ref[d, a * BIAS_BLOCK:(a + 1) * BIAS_BLOCK, c * BIAS_BLOCK:(c + 1) * BIAS_BLOCK] = (
                    near[e] if 0 <= e < 2 else const[e < 0])
    bm_ref[...] = jnp.zeros(bm_ref.shape, F32)
    bm_ref[:BIAS_BLOCK, :] = build(idxm_ref[...])
    lv = lv_ref[...]
    e1 = jnp.exp(jnp.sum(lv[0:1] * lv[1:2], axis=-1, keepdims=True))
    e2 = jnp.exp(jnp.sum(lv[2:3] * lv[3:4], axis=-1, keepdims=True))
    lam_ref[...] = e1 - e2 + lambda_init


def _bias_tables(rel_bias, lam_vecs, *, tq, lambda_init):
    n_heads = rel_bias.shape[1]
    assert tq % BIAS_BLOCK == 0
    qi = np.arange(BIAS_BLOCK)[:, None]
    idx_d = np.stack([_bucket_table(e * BIAS_BLOCK + qi - np.arange(BIAS_BLOCK)[None, :]) for e in range(2)])
    idx_m = _bucket_table(N_META + qi - np.arange(N_META)[None, :])
    kern = functools.partial(_bias_kernel, lambda_init=lambda_init)
    return pl.pallas_call(
        kern,
        out_shape=(jax.ShapeDtypeStruct((n_heads, 2, tq, tq), F32),
                   jax.ShapeDtypeStruct((n_heads, tq, N_META), F32),
                   jax.ShapeDtypeStruct((1, 1), F32)),
        grid=(n_heads,),
        in_specs=[
            pl.BlockSpec(memory_space=pltpu.SMEM),
            pl.BlockSpec(lam_vecs.shape, lambda h: (0, 0)),
            pl.BlockSpec(idx_d.shape, lambda h: (0, 0, 0)),
            pl.BlockSpec(idx_m.shape, lambda h: (0, 0)),
        ],
        out_specs=(pl.BlockSpec((None, 2, tq, tq), lambda h: (h, 0, 0, 0)),
                   pl.BlockSpec((None, tq, N_META), lambda h: (h, 0, 0)),
                   pl.BlockSpec((1, 1), lambda h: (0, 0))),
        compiler_params=pltpu.CompilerParams(dimension_semantics=("arbitrary",)),
        name="rel_bias_tiles",
    )(rel_bias, lam_vecs, jnp.asarray(idx_d), jnp.asarray(idx_m))


def _attn_kernel(lam_ref, q_ref, k_ref, v_ref, km_ref, vm_ref, bd_ref, bm_ref, g_ref,
                 o_ref, vext_ref, qs_ref, s_ref, p_ref, m_ref, acc_ref, *, tq, lambda_init, heads):
    n_q = q_ref.shape[0] // tq
    ones = jnp.ones((1, V_DIM), BF16)
    lanes = [slice(u * V_DIM, (u + 1) * V_DIM) for u in range(heads)]
    lane = lax.broadcasted_iota(jnp.int32, (tq, V_DIM), 1)

    for u in range(heads):
        vext_ref[u, :, :V_DIM] = v_ref[:, lanes[u]]
        vext_ref[u, :, V_DIM:] = jnp.broadcast_to(ones, (v_ref.shape[0], V_DIM))

    def both(b):
        return jnp.concatenate([b, b], axis=0)

    def row_max(s):
        return jnp.broadcast_to(jnp.max(s, axis=-1, keepdims=True), (s.shape[0], V_DIM))

    def wide(r, like):
        return jnp.concatenate([r] * (like.shape[1] // V_DIM), axis=1)

    def scores(u, tile):
        off = pl.multiple_of(tile * tq, tq)
        return lax.dot_general(qs_ref[u], k_ref[pl.ds(off, tq), lanes[u]], _NT, preferred_element_type=F32)

    def weighted_values(u, tile):
        off = pl.multiple_of(tile * tq, tq)
        return jnp.dot(p_ref[u], vext_ref[u, pl.ds(off, tq), :], preferred_element_type=F32)

    def start(i):
        rows = pl.ds(pl.multiple_of(i * tq, tq), tq)
        for u in range(heads):
            q = q_ref[rows, lanes[u]]
            zero = jnp.zeros_like(q)
            qs_ref[u] = jnp.concatenate([jnp.where(lane < HEAD_DIM, q, zero),
                                         jnp.where(lane >= HEAD_DIM, q, zero)], axis=0)
            s_ref[u] = scores(u, i) + both(bd_ref[u, 0])

    def first_step(i, first_tile):
        for u in range(heads):
            s_meta = lax.dot_general(qs_ref[u], km_ref[:, lanes[u]], _NT, preferred_element_type=F32)
            if first_tile:
                s_meta = s_meta + both(bm_ref[u])
            else:
                s_next = scores(u, i - 1) + both(bd_ref[u, 1])
            s = s_ref[u]
            m = jnp.maximum(row_max(s), row_max(s_meta))
            p_meta = jnp.exp2(s_meta - m[:, :N_META]).astype(BF16)
            p_ref[u] = jnp.exp2(s - wide(m, s)).astype(BF16)
            m_ref[u] = m
            if not first_tile:
                s_ref[u] = s_next
            vm_ext = jnp.concatenate([vm_ref[:, lanes[u]], jnp.broadcast_to(ones, (N_META, V_DIM))], axis=-1)
            acc_ref[u] = jnp.dot(p_meta, vm_ext, preferred_element_type=F32)

    def step(next_tile, prev_tile):
        for u in range(heads):
            s_next = scores(u, next_tile)
            pv = weighted_values(u, prev_tile)
            s = s_ref[u]
            m_old = m_ref[u]
            m_new = jnp.maximum(m_old, row_max(s))
            p_ref[u] = jnp.exp2(s - wide(m_new, s)).astype(BF16)
            m_ref[u] = m_new
            s_ref[u] = s_next
            acc_ref[u] = (acc_ref[u] + pv) * wide(jnp.exp2(m_old - m_new), pv)

    def finish(i, last_step):
        rows = pl.ds(pl.multiple_of(i * tq, tq), tq)
        for u in range(heads):
            if last_step:
                pv = weighted_values(u, 1)
                s = s_ref[u]
                m_old = m_ref[u]
                m_new = jnp.maximum(m_old, row_max(s))
                p = jnp.exp2(s - wide(m_new, s)).astype(BF16)
                acc = (acc_ref[u] + pv) * wide(jnp.exp2(m_old - m_new), pv)
                acc = acc + jnp.dot(p, vext_ref[u, :tq, :], preferred_element_type=F32)
            else:
                acc = acc_ref[u] + weighted_values(u, 0)
            o = acc[:, :V_DIM] / acc[:, V_DIM:]
            o = o[:tq] - lam_ref[0, 0] * o[tq:]
            o = _rms(o, g_ref[...]) * (1.0 - lambda_init)
            o_ref[rows, lanes[u]] = o.astype(BF16)

    start(0)

    @pl.loop(0, n_q)
    def _(i):
        for first_tile in (True, False):
            @pl.when((i == 0) == first_tile)
            def _():
                first_step(i, first_tile)

        @pl.loop(1, i)
        def _(t):
            step(i - t - 1, i - t + 1)

        for first_tile, more in ((True, n_q > 1), (False, True), (False, False)):
            @pl.when(((i == 0) == first_tile) & ((i < n_q - 1) == more))
            def _():
                finish(i, last_step=not first_tile)
                if more:
                    start(i + 1)


def _attention(qkv, qkv_meta, lam, bias_d, bias_m, subln_g, *, tq, lambda_init, heads):
    b, s, d3 = qkv.shape
    d = d3 // 3
    n_groups = d // (V_DIM * heads)
    w = V_DIM * heads
    kern = functools.partial(_attn_kernel, tq=tq, lambda_init=lambda_init, heads=heads)
    return pl.pallas_call(
        kern,
        out_shape=jax.ShapeDtypeStruct((b, s, d), BF16),
        grid=(b, n_groups),
        in_specs=[
            pl.BlockSpec(memory_space=pltpu.SMEM),
            pl.BlockSpec((None, s, w), lambda b, g: (b, 0, g)),
            pl.BlockSpec((None, s, w), lambda b, g: (b, 0, n_groups + g)),
            pl.BlockSpec((None, s, w), lambda b, g: (b, 0, 2 * n_groups + g)),
            pl.BlockSpec((N_META, w), lambda b, g: (0, n_groups + g)),
            pl.BlockSpec((N_META, w), lambda b, g: (0, 2 * n_groups + g)),
            pl.BlockSpec((heads, 2, tq, tq), lambda b, g: (g, 0, 0, 0)),
            pl.BlockSpec((heads, tq, N_META), lambda b, g: (g, 0, 0)),
            pl.BlockSpec((1, V_DIM), lambda b, g: (0, 0)),
        ],
        out_specs=pl.BlockSpec((None, s, w), lambda b, g: (b, 0, g)),
        scratch_shapes=[
            pltpu.VMEM((heads, s, 2 * V_DIM), BF16),
            pltpu.VMEM((heads, 2 * tq, V_DIM), BF16),
            pltpu.VMEM((heads, 2 * tq, tq), F32),
            pltpu.VMEM((heads, 2 * tq, tq), BF16),
            pltpu.VMEM((heads, 2 * tq, V_DIM), F32),
            pltpu.VMEM((heads, 2 * tq, 2 * V_DIM), F32),
        ],
        compiler_params=pltpu.CompilerParams(
            dimension_semantics=("arbitrary", "arbitrary"),
            vmem_limit_bytes=V7X_VMEM_LIMIT),
        name="diff_attention",
    )(lam, qkv, qkv, qkv, qkv_meta, qkv_meta, bias_d, bias_m, subln_g)


def _layer1_kernel(a_ref, h_ref, wo_ref, g2_ref, wg_ref, wu_ref, wd_ref, gf_ref, o_ref, *, fc):
    sub = min(a_ref.shape[0], SUB_ROWS)
    n_sub = a_ref.shape[0] // sub

    def prepare(c):
        r = slice(c * sub, (c + 1) * sub)
        st = {"rows": r}

        def out_proj():
            st["hmid"] = h_ref[r, :] + jnp.dot(a_ref[r, :], wo_ref[...], preferred_element_type=F32)
            st["hn2"] = _rms(st["hmid"], g2_ref[...]).astype(BF16)

        return [out_proj], st

    def finalize(st, ffn):
        o_ref[st["rows"], :] = _rms(st["hmid"] + ffn, gf_ref[...])

    stages, st = prepare(0)
    stages[0]()
    pending = []
    for c in range(n_sub):
        nxt_stages, nxt = prepare(c + 1) if c + 1 < n_sub else ([], None)
        ffn = _interleave(_swiglu_steps(st["hn2"], wg_ref, wu_ref, wd_ref, fc), pending + nxt_stages)
        pending = [functools.partial(finalize, st, ffn)]
        st = nxt
    pending[0]()


def _layer1(a, h, wo, g2, wg, wu, wd, gf, *, layer, rows, fc):
    n, d = h.shape
    f = wg.shape[2]
    kern = functools.partial(_layer1_kernel, fc=fc)
    return pl.pallas_call(
        kern,
        out_shape=jax.ShapeDtypeStruct((n, d), F32),
        grid=(n // rows,),
        in_specs=[
            pl.BlockSpec((rows, d), lambda i: (i, 0)),
            pl.BlockSpec((rows, d), lambda i: (i, 0)),
            _resident((d, d)),
            _resident((1, d)),
            _resident((d, f), layer),
            _resident((d, f), layer),
            _resident((f, d), layer),
            _resident((1, d)),
        ],
        out_specs=pl.BlockSpec((rows, d), lambda i: (i, 0)),
        compiler_params=pltpu.CompilerParams(
            dimension_semantics=("arbitrary",), vmem_limit_bytes=V7X_VMEM_LIMIT),
        name="layer1_out_swiglu_norm",
    )(a, h, wo, g2, wg, wu, wd, gf)


ATTN_HEADS_PER_STEP = 2


def _pick_rows(seq, want):
    rows = min(seq, want)
    assert seq % rows == 0 and rows % HALO == 0
    return rows


def kernel(x, meta_tokens, rel_bias, mix_norm_g, ffn_norm_g, pool_w, pool_scale, attn_w_qkv, attn_w_o,
           lambda_q1, lambda_k1, lambda_q2, lambda_k2, subln_g, ffn_w_gate, ffn_w_up, ffn_w_down,
           final_norm_g):
    bsz, seq, d = x.shape
    assert meta_tokens.shape == (N_META, d) and N_META == HALO
    rows = _pick_rows(seq, 1024)
    tq = _pick_rows(seq, 512)
    fc = 256
    lambda_init = 0.8 - 0.6 * math.exp(-0.3 * 1)

    row = lambda v: v.reshape(1, -1)
    wg, wu, wd = (w.astype(BF16) for w in (ffn_w_gate, ffn_w_up, ffn_w_down))
    pw = pool_w[0].astype(BF16)
    wqkv = attn_w_qkv[0].astype(BF16)
    wo = attn_w_o[0].astype(BF16)
    xr = x.reshape(bsz * seq, d)

    l0 = functools.partial(_layer0, g1=row(mix_norm_g[0]), pw=pw, ps=row(pool_scale[0]),
                           g2=row(ffn_norm_g[0]), wg=wg, wu=wu, wd=wd, layer=0, fc=fc)
    h1 = l0(xr, meta_tokens, rows=rows, tiles_per_seq=seq // rows, pos0=N_META)
    h1_meta = l0(meta_tokens, jnp.zeros_like(meta_tokens), rows=N_META, tiles_per_seq=1, pos0=0)

    qkv = _qkv(h1, row(mix_norm_g[1]), wqkv, rows=rows)
    qkv_meta = _qkv(h1_meta, row(mix_norm_g[1]), wqkv, rows=N_META)

    lam_vecs = jnp.stack([lambda_q1[0], lambda_k1[0], lambda_q2[0], lambda_k2[0]])
    bias_d, bias_m, lam = _bias_tables(rel_bias, lam_vecs, tq=tq, lambda_init=lambda_init)
    attn = _attention(qkv.reshape(bsz, seq, 3 * d), qkv_meta, lam, bias_d, bias_m,
                      row(subln_g[0]), tq=tq, lambda_init=lambda_init, heads=ATTN_HEADS_PER_STEP)

    out = _layer1(attn.reshape(bsz * seq, d), h1, wo, row(ffn_norm_g[1]), wg, wu, wd,
                  row(final_norm_g), layer=1, rows=rows, fc=fc)
    return out.reshape(bsz, seq, d)
```

```python
import functools
import math

import numpy as np
import jax
import jax.numpy as jnp
from jax import lax
from jax.experimental import pallas as pl
from jax.experimental.pallas import tpu as pltpu

N_META = 16
POOL_WINDOWS = (2, 4, 8, 16)
HEAD_DIM = 64
V_DIM = 2 * HEAD_DIM
N_BUCKETS = 32
MAX_DISTANCE = 128
RMS_EPS = 1e-6
NEG_INF = -1e30
LOG2E = math.log2(math.e)
HALO = 16
SUB_ROWS = 512
BIAS_BLOCK = 128
V7X_VMEM_LIMIT = 56 * 1024 * 1024

F32 = jnp.float32
BF16 = jnp.bfloat16
_NT = (((1,), (1,)), ((), ()))


def _rms(x, g):
    ms = jnp.mean(x * x, axis=-1, keepdims=True)
    return x * lax.rsqrt(ms + RMS_EPS) * g


def _swiglu_steps(hn, wg_ref, wu_ref, wd_ref, fc):
    acc = None
    for c in range(wg_ref.shape[1] // fc):
        sl = slice(c * fc, (c + 1) * fc)
        g = jnp.dot(hn, wg_ref[:, sl], preferred_element_type=F32)
        u = jnp.dot(hn, wu_ref[:, sl], preferred_element_type=F32)
        a = (g * jax.nn.sigmoid(g) * u).astype(BF16)
        d = jnp.dot(a, wd_ref[sl, :], preferred_element_type=F32)
        acc = d if acc is None else acc + d
        yield acc


def _interleave(steps, stages):
    stages = list(stages)
    done = 0
    for k, value in enumerate(steps):
        while stages and done * 2 <= k:
            stages.pop(0)()
            done += 1
    for stage in stages:
        stage()
    return value


def _swiglu(hn, wg_ref, wu_ref, wd_ref, fc):
    return _interleave(_swiglu_steps(hn, wg_ref, wu_ref, wd_ref, fc), [])


def _layer0_kernel(x_ref, xnext_ref, first_ref, g1_ref, pw_ref, ps_ref, g2_ref,
                   wg_ref, wu_ref, wd_ref, o_ref, hmid_ref, hn2_ref, *, tiles_per_seq, pos0, fc, sub):
    i = pl.program_id(0)
    rows = x_ref.shape[0]
    n_sub = rows // sub
    gc = pw_ref.shape[1]
    g1 = g1_ref[...]
    slot = i % 2

    def mixer_stages(load_x, halo, pos_base):
        st = {"mix": []}

        def norm():
            st["x"] = load_x()
            st["hn"] = _rms(st["x"], g1)
            st["full"] = jnp.concatenate([_rms(halo(), g1), st["hn"]], axis=0)
            st["pos"] = lax.broadcasted_iota(jnp.int32, (sub, 1), 0) + pos_base

        def pool(g, w):
            cols = slice(g * gc, (g + 1) * gc)
            s = st["full"][:, cols]
            k = 1
            while k < w:
                s = s + pltpu.roll(s, k, 0)
                k *= 2
            inv_cnt = 1.0 / jnp.minimum(st["pos"] + 1, w).astype(F32)
            pooled = s[HALO:] * inv_cnt - st["hn"][:, cols]
            st["mix"].append(jnp.dot(pooled.astype(BF16), pw_ref[g], preferred_element_type=F32))

        def residual():
            st["hmid"] = st["x"] + jnp.concatenate(st["mix"], axis=-1) * ps_ref[...]
            st["hn2"] = _rms(st["hmid"], g2_ref[...]).astype(BF16)

        stages = [norm] + [functools.partial(pool, g, w) for g, w in enumerate(POOL_WINDOWS)] + [residual]
        return stages, st

    @pl.when(i == 0)
    def _():
        stages, st = mixer_stages(lambda: x_ref[:sub, :], lambda: first_ref[...], pos0)
        for stage in stages:
            stage()
        hmid_ref[0] = st["hmid"]
        hn2_ref[0] = st["hn2"]

    tile_in_seq = i % tiles_per_seq
    next_in_seq = (i + 1) % tiles_per_seq
    st = None
    for c in range(n_sub):
        r = slice(c * sub, (c + 1) * sub)
        if c + 1 < n_sub:
            nr = slice((c + 1) * sub, (c + 2) * sub)
            nxt_stages, nxt = mixer_stages(lambda nr=nr: x_ref[nr, :],
                                           lambda nr=nr: x_ref[nr.start - HALO:nr.start, :],
                                           tile_in_seq * rows + nr.start + pos0)
        else:
            nxt_stages, nxt = mixer_stages(
                lambda: xnext_ref[...],
                lambda: jnp.where(next_in_seq == 0, first_ref[...], x_ref[rows - HALO:, :]),
                next_in_seq * rows + pos0)

            def hand_over(nxt=nxt):
                hmid_ref[1 - slot] = nxt["hmid"]
                hn2_ref[1 - slot] = nxt["hn2"]

            nxt_stages = nxt_stages + [hand_over]
        hn2 = hn2_ref[slot] if c == 0 else st["hn2"]
        ffn = _interleave(_swiglu_steps(hn2, wg_ref, wu_ref, wd_ref, fc), nxt_stages)
        o_ref[r, :] = (hmid_ref[slot] if c == 0 else st["hmid"]) + ffn
        st = nxt


def _resident(shape, layer=None):
    if layer is None:
        return pl.BlockSpec(shape, lambda *_: (0,) * len(shape), pipeline_mode=pl.Buffered(1))
    return pl.BlockSpec((None,) + shape, lambda *_: (layer,) + (0,) * len(shape), pipeline_mode=pl.Buffered(1))


def _layer0(x, first, g1, pw, ps, g2, wg, wu, wd, *, layer, rows, tiles_per_seq, pos0, fc):
    n, d = x.shape
    f = wg.shape[2]
    sub = min(rows, SUB_ROWS)
    n_sub = rows // sub
    kern = functools.partial(_layer0_kernel, tiles_per_seq=tiles_per_seq, pos0=pos0, fc=fc, sub=sub)
    return pl.pallas_call(
        kern,
        out_shape=jax.ShapeDtypeStruct((n, d), F32),
        grid=(n // rows,),
        in_specs=[
            pl.BlockSpec((rows, d), lambda i: (i, 0)),
            pl.BlockSpec((sub, d), lambda i: (jnp.minimum((i + 1) * n_sub, n // sub - 1), 0)),
            _resident((HALO, d)),
            _resident((1, d)),
            _resident(pw.shape),
            _resident((1, d)),
            _resident((1, d)),
            _resident((d, f), layer),
            _resident((d, f), layer),
            _resident((f, d), layer),
        ],
        out_specs=pl.BlockSpec((rows, d), lambda i: (i, 0)),
        scratch_shapes=[pltpu.VMEM((2, sub, d), F32),
                        pltpu.VMEM((2, sub, d), BF16)],
        compiler_params=pltpu.CompilerParams(
            dimension_semantics=("arbitrary",), vmem_limit_bytes=V7X_VMEM_LIMIT),
        name="layer0_pool_swiglu",
    )(x, x, first, g1, pw, ps, g2, wg, wu, wd)


def _qkv_kernel(h_ref, g_ref, w_ref, o_ref):
    d = h_ref.shape[1]
    hn = _rms(h_ref[...], g_ref[...]).astype(BF16)
    for c in range(3):
        sl = slice(c * d, (c + 1) * d)
        y = jnp.dot(hn, w_ref[:, sl], preferred_element_type=F32)
        if c == 0:
            y = y * (LOG2E / math.sqrt(HEAD_DIM))
        o_ref[:, sl] = y.astype(BF16)


def _qkv(h, g, w, *, rows):
    n, d = h.shape
    return pl.pallas_call(
        _qkv_kernel,
        out_shape=jax.ShapeDtypeStruct((n, 3 * d), BF16),
        grid=(n // rows,),
        in_specs=[
            pl.BlockSpec((rows, d), lambda i: (i, 0)),
            _resident((1, d)),
            _resident((d, 3 * d)),
        ],
        out_specs=pl.BlockSpec((rows, 3 * d), lambda i: (i, 0)),
        compiler_params=pltpu.CompilerParams(
            dimension_semantics=("arbitrary",), vmem_limit_bytes=V7X_VMEM_LIMIT),
        name="qkv_proj",
    )(h, g, w)


def _bucket_table(rel):
    n = np.maximum(rel, 0)
    max_exact = N_BUCKETS // 2
    nf = np.maximum(n, max_exact).astype(np.float32)
    large = max_exact + (np.log(nf / np.float32(max_exact)) / np.float32(math.log(MAX_DISTANCE / max_exact))
                         * np.float32(N_BUCKETS - max_exact)).astype(np.int32)
    large = np.minimum(large, N_BUCKETS - 1)
    return np.where(rel < 0, -1, np.where(n < max_exact, n, large)).astype(np.int32)


def _bias_kernel(rb_ref, lv_ref, idxd_ref, idxm_ref, bd_ref, bm_ref, lam_ref, *, lambda_init):
    h = pl.program_id(0)
    far = rb_ref[N_BUCKETS - 1, h]

    def build(idx):
        out = jnp.full(idx.shape, NEG_INF, F32)
        for b in range(N_BUCKETS):
            out = jnp.where(idx == b, (rb_ref[b, h] - far) * LOG2E, out)
        return out

    near = [build(idxd_ref[e]) for e in range(2)]
    const = {True: jnp.full((BIAS_BLOCK, BIAS_BLOCK), NEG_INF, F32),
             False: jnp.zeros((BIAS_BLOCK, BIAS_BLOCK), F32)}
    blocks = bd_ref.shape[1] // BIAS_BLOCK
    for d in range(bd_ref.shape[0]):
        for a in range(blocks):
            for c in range(blocks):
                e = d * blocks + a - c
                bd_ref[d, a * BIAS_BLOCK:(a + 1) * BIAS_BLOCK, c * BIAS_BLOCK:(c + 1) * BIAS_BLOCK] = (
                    near[e] if 0 <= e < 2 else const[e < 0])
    bm_ref[...] = jnp.zeros(bm_ref.shape, F32)
    bm_ref[:BIAS_BLOCK, :] = build(idxm_ref[...])
    lv = lv_ref[...]
    e1 = jnp.exp(jnp.sum(lv[0:1] * lv[1:2], axis=-1, keepdims=True))
    e2 = jnp.exp(jnp.sum(lv[2:3] * lv[3:4], axis=-1, keepdims=True))
    lam_ref[...] = e1 - e2 + lambda_init


def _bias_tables(rel_bias, lam_vecs, *, tq, lambda_init):
    n_heads = rel_bias.shape[1]
    assert tq % BIAS_BLOCK == 0
    qi = np.arange(BIAS_BLOCK)[:, None]
    idx_d = np.stack([_bucket_table(e * BIAS_BLOCK + qi - np.arange(BIAS_BLOCK)[None, :]) for e in range(2)])
    idx_m = _bucket_table(N_META + qi - np.arange(N_META)[None, :])
    kern = functools.partial(_bias_kernel, lambda_init=lambda_init)
    return pl.pallas_call(
        kern,
        out_shape=(jax.ShapeDtypeStruct((n_heads, 2, tq, tq), F32),
                   jax.ShapeDtypeStruct((n_heads, tq, N_META), F32),
                   jax.ShapeDtypeStruct((1, 1), F32)),
        grid=(n_heads,),
        in_specs=[
            pl.BlockSpec(memory_space=pltpu.SMEM),
            pl.BlockSpec(lam_vecs.shape, lambda h: (0, 0)),
            pl.BlockSpec(idx_d.shape, lambda h: (0, 0, 0)),
            pl.BlockSpec(idx_m.shape, lambda h: (0, 0)),
        ],
        out_specs=(pl.BlockSpec((None, 2, tq, tq), lambda h: (h, 0, 0, 0)),
                   pl.BlockSpec((None, tq, N_META), lambda h: (h, 0, 0)),
                   pl.BlockSpec((1, 1), lambda h: (0, 0))),
        compiler_params=pltpu.CompilerParams(dimension_semantics=("arbitrary",)),
        name="rel_bias_tiles",
    )(rel_bias, lam_vecs, jnp.asarray(idx_d), jnp.asarray(idx_m))


def _attn_kernel(lam_ref, q_ref, k_ref, v_ref, km_ref, vm_ref, bd_ref, bm_ref, g_ref,
                 o_ref, vext_ref, qs_ref, s_ref, m_ref, acc_ref, *, tq, lambda_init, heads):
    n_q = q_ref.shape[0] // tq
    ones = jnp.ones((1, V_DIM), BF16)
    lanes = [slice(u * V_DIM, (u + 1) * V_DIM) for u in range(heads)]
    lane = lax.broadcasted_iota(jnp.int32, (tq, V_DIM), 1)

    for u in range(heads):
        vext_ref[u, :, :V_DIM] = v_ref[:, lanes[u]]
        vext_ref[u, :, V_DIM:] = jnp.broadcast_to(ones, (v_ref.shape[0], V_DIM))

    def both(b):
        return jnp.concatenate([b, b], axis=0)

    def row_max(s):
        return jnp.broadcast_to(jnp.max(s, axis=-1, keepdims=True), (s.shape[0], V_DIM))

    def wide(r, like):
        return jnp.concatenate([r] * (like.shape[1] // V_DIM), axis=1)

    def scores(u, tile):
        off = pl.multiple_of(tile * tq, tq)
        return lax.dot_general(qs_ref[u], k_ref[pl.ds(off, tq), lanes[u]], _NT, preferred_element_type=F32)

    def weighted_values(u, p, tile):
        off = pl.multiple_of(tile * tq, tq)
        return jnp.dot(p, vext_ref[u, pl.ds(off, tq), :], preferred_element_type=F32)

    def start(i):
        rows = pl.ds(pl.multiple_of(i * tq, tq), tq)
        for u in range(heads):
            q = q_ref[rows, lanes[u]]
            zero = jnp.zeros_like(q)
            qs_ref[u] = jnp.concatenate([jnp.where(lane < HEAD_DIM, q, zero),
                                         jnp.where(lane >= HEAD_DIM, q, zero)], axis=0)
            s_ref[u] = scores(u, i) + both(bd_ref[u, 0])

    def first_step(i, first_tile):
        for u in range(heads):
            s_meta = lax.dot_general(qs_ref[u], km_ref[:, lanes[u]], _NT, preferred_element_type=F32)
            if first_tile:
                s_meta = s_meta + both(bm_ref[u])
            else:
                s_next = scores(u, i - 1) + both(bd_ref[u, 1])
            s = s_ref[u]
            m = jnp.maximum(row_max(s), row_max(s_meta))
            p_meta = jnp.exp2(s_meta - m[:, :N_META]).astype(BF16)
            p = jnp.exp2(s - wide(m, s)).astype(BF16)
            m_ref[u] = m
            if not first_tile:
                s_ref[u] = s_next
            vm_ext = jnp.concatenate([vm_ref[:, lanes[u]], jnp.broadcast_to(ones, (N_META, V_DIM))], axis=-1)
            acc_ref[u] = (jnp.dot(p_meta, vm_ext, preferred_element_type=F32)
                          + weighted_values(u, p, i))

    def step(next_tile, tile):
        for u in range(heads):
            s_next = scores(u, next_tile)
            s = s_ref[u]
            m_old = m_ref[u]
            m_new = jnp.maximum(m_old, row_max(s))
            p = jnp.exp2(s - wide(m_new, s)).astype(BF16)
            m_ref[u] = m_new
            s_ref[u] = s_next
            pv = weighted_values(u, p, tile)
            acc_ref[u] = acc_ref[u] * wide(jnp.exp2(m_old - m_new), pv) + pv

    def finish(i, last_step):
        rows = pl.ds(pl.multiple_of(i * tq, tq), tq)
        for u in range(heads):
            acc = acc_ref[u]
            if last_step:
                s = s_ref[u]
                m_old = m_ref[u]
                m_new = jnp.maximum(m_old, row_max(s))
                p = jnp.exp2(s - wide(m_new, s)).astype(BF16)
                acc = acc * wide(jnp.exp2(m_old - m_new), acc) + weighted_values(u, p, 0)
            o = acc[:, :V_DIM] / acc[:, V_DIM:]
            o = o[:tq] - lam_ref[0, 0] * o[tq:]
            o = _rms(o, g_ref[...]) * (1.0 - lambda_init)
            o_ref[rows, lanes[u]] = o.astype(BF16)

    start(0)

    @pl.loop(0, n_q)
    def _(i):
        for first_tile in (True, False):
            @pl.when((i == 0) == first_tile)
            def _():
                first_step(i, first_tile)

        @pl.loop(1, i)
        def _(t):
            step(i - t - 1, i - t)

        for first_tile, more in ((True, n_q > 1), (False, True), (False, False)):
            @pl.when(((i == 0) == first_tile) & ((i < n_q - 1) == more))
            def _():
                finish(i, last_step=not first_tile)
                if more:
                    start(i + 1)


def _attention(qkv, qkv_meta, lam, bias_d, bias_m, subln_g, *, tq, lambda_init, heads):
    b, s, d3 = qkv.shape
    d = d3 // 3
    n_groups = d // (V_DIM * heads)
    w = V_DIM * heads
    kern = functools.partial(_attn_kernel, tq=tq, lambda_init=lambda_init, heads=heads)
    return pl.pallas_call(
        kern,
        out_shape=jax.ShapeDtypeStruct((b, s, d), BF16),
        grid=(b, n_groups),
        in_specs=[
            pl.BlockSpec(memory_space=pltpu.SMEM),
            pl.BlockSpec((None, s, w), lambda b, g: (b, 0, g)),
            pl.BlockSpec((None, s, w), lambda b, g: (b, 0, n_groups + g)),
            pl.BlockSpec((None, s, w), lambda b, g: (b, 0, 2 * n_groups + g)),
            pl.BlockSpec((N_META, w), lambda b, g: (0, n_groups + g)),
            pl.BlockSpec((N_META, w), lambda b, g: (0, 2 * n_groups + g)),
            pl.BlockSpec((heads, 2, tq, tq), lambda b, g: (g, 0, 0, 0)),
            pl.BlockSpec((heads, tq, N_META), lambda b, g: (g, 0, 0)),
            pl.BlockSpec((1, V_DIM), lambda b, g: (0, 0)),
        ],
        out_specs=pl.BlockSpec((None, s, w), lambda b, g: (b, 0, g)),
        scratch_shapes=[
            pltpu.VMEM((heads, s, 2 * V_DIM), BF16),
            pltpu.VMEM((heads, 2 * tq, V_DIM), BF16),
            pltpu.VMEM((heads, 2 * tq, tq), F32),
            pltpu.VMEM((heads, 2 * tq, V_DIM), F32),
            pltpu.VMEM((heads, 2 * tq, 2 * V_DIM), F32),
        ],
        compiler_params=pltpu.CompilerParams(
            dimension_semantics=("arbitrary", "arbitrary"),
            vmem_limit_bytes=V7X_VMEM_LIMIT),
        name="diff_attention",
    )(lam, qkv, qkv, qkv, qkv_meta, qkv_meta, bias_d, bias_m, subln_g)


def _layer1_kernel(a_ref, h_ref, wo_ref, g2_ref, wg_ref, wu_ref, wd_ref, gf_ref, o_ref, *, fc):
    sub = min(a_ref.shape[0], SUB_ROWS)
    n_sub = a_ref.shape[0] // sub

    def prepare(c):
        r = slice(c * sub, (c + 1) * sub)
        st = {"rows": r}

        def out_proj():
            st["hmid"] = h_ref[r, :] + jnp.dot(a_ref[r, :], wo_ref[...], preferred_element_type=F32)
            st["hn2"] = _rms(st["hmid"], g2_ref[...]).astype(BF16)

        return [out_proj], st

    def finalize(st, ffn):
        o_ref[st["rows"], :] = _rms(st["hmid"] + ffn, gf_ref[...])

    stages, st = prepare(0)
    stages[0]()
    pending = []
    for c in range(n_sub):
        nxt_stages, nxt = prepare(c + 1) if c + 1 < n_sub else ([], None)
        ffn = _interleave(_swiglu_steps(st["hn2"], wg_ref, wu_ref, wd_ref, fc), pending + nxt_stages)
        pending = [functools.partial(finalize, st, ffn)]
        st = nxt
    pending[0]()


def _layer1(a, h, wo, g2, wg, wu, wd, gf, *, layer, rows, fc):
    n, d = h.shape
    f = wg.shape[2]
    kern = functools.partial(_layer1_kernel, fc=fc)
    return pl.pallas_call(
        kern,
        out_shape=jax.ShapeDtypeStruct((n, d), F32),
        grid=(n // rows,),
        in_specs=[
            pl.BlockSpec((rows, d), lambda i: (i, 0)),
            pl.BlockSpec((rows, d), lambda i: (i, 0)),
            _resident((d, d)),
            _resident((1, d)),
            _resident((d, f), layer),
            _resident((d, f), layer),
            _resident((f, d), layer),
            _resident((1, d)),
        ],
        out_specs=pl.BlockSpec((rows, d), lambda i: (i, 0)),
        compiler_params=pltpu.CompilerParams(
            dimension_semantics=("arbitrary",), vmem_limit_bytes=V7X_VMEM_LIMIT),
        name="layer1_out_swiglu_norm",
    )(a, h, wo, g2, wg, wu, wd, gf)


ATTN_HEADS_PER_STEP = 2


def _pick_rows(seq, want):
    rows = min(seq, want)
    assert seq % rows == 0 and rows % HALO == 0
    return rows


def kernel(x, meta_tokens, rel_bias, mix_norm_g, ffn_norm_g, pool_w, pool_scale, attn_w_qkv, attn_w_o,
           lambda_q1, lambda_k1, lambda_q2, lambda_k2, subln_g, ffn_w_gate, ffn_w_up, ffn_w_down,
           final_norm_g):
    bsz, seq, d = x.shape
    assert meta_tokens.shape == (N_META, d) and N_META == HALO
    rows = _pick_rows(seq, 1024)
    tq = _pick_rows(seq, 512)
    fc = 256
    lambda_init = 0.8 - 0.6 * math.exp(-0.3 * 1)

    row = lambda v: v.reshape(1, -1)
    wg, wu, wd = (w.astype(BF16) for w in (ffn_w_gate, ffn_w_up, ffn_w_down))
    pw = pool_w[0].astype(BF16)
    wqkv = attn_w_qkv[0].astype(BF16)
    wo = attn_w_o[0].astype(BF16)
    xr = x.reshape(bsz * seq, d)

    l0 = functools.partial(_layer0, g1=row(mix_norm_g[0]), pw=pw, ps=row(pool_scale[0]),
                           g2=row(ffn_norm_g[0]), wg=wg, wu=wu, wd=wd, layer=0, fc=fc)
    h1 = l0(xr, meta_tokens, rows=rows, tiles_per_seq=seq // rows, pos0=N_META)
    h1_meta = l0(meta_tokens, jnp.zeros_like(meta_tokens), rows=N_META, tiles_per_seq=1, pos0=0)

    qkv = _qkv(h1, row(mix_norm_g[1]), wqkv, rows=rows)
    qkv_meta = _qkv(h1_meta, row(mix_norm_g[1]), wqkv, rows=N_META)

    lam_vecs = jnp.stack([lambda_q1[0], lambda_k1[0], lambda_q2[0], lambda_k2[0]])
    bias_d, bias_m, lam = _bias_tables(rel_bias, lam_vecs, tq=tq, lambda_init=lambda_init)
    attn = _attention(qkv.reshape(bsz, seq, 3 * d), qkv_meta, lam, bias_d, bias_m,
                      row(subln_g[0]), tq=tq, lambda_init=lambda_init, heads=ATTN_HEADS_PER_STEP)

    out = _layer1(attn.reshape(bsz * seq, d), h1, wo, row(ffn_norm_g[1]), wg, wu, wd,
                  row(final_norm_g), layer=1, rows=rows, fc=fc)
    return out.reshape(bsz, seq, d)
```

```python
import functools
import math

import numpy as np
import jax
import jax.numpy as jnp
from jax import lax
from jax.experimental import pallas as pl
from jax.experimental.pallas import tpu as pltpu

N_META = 16
POOL_WINDOWS = (2, 4, 8, 16)
HEAD_DIM = 64
V_DIM = 2 * HEAD_DIM
N_BUCKETS = 32
MAX_DISTANCE = 128
RMS_EPS = 1e-6
NEG_INF = -1e30
LOG2E = math.log2(math.e)
HALO = 16
SUB_ROWS = 512
FINISH_ROWS = 128
BIAS_BLOCK = 128
V7X_VMEM_LIMIT = 56 * 1024 * 1024

F32 = jnp.float32
BF16 = jnp.bfloat16
_NT = (((1,), (1,)), ((), ()))


def _rms(x, g):
    ms = jnp.mean(x * x, axis=-1, keepdims=True)
    return x * lax.rsqrt(ms + RMS_EPS) * g


def _swiglu_steps(hn, wg_ref, wu_ref, wd_ref, fc):
    acc = None
    for c in range(wg_ref.shape[1] // fc):
        sl = slice(c * fc, (c + 1) * fc)
        g = jnp.dot(hn, wg_ref[:, sl], preferred_element_type=F32)
        u = jnp.dot(hn, wu_ref[:, sl], preferred_element_type=F32)
        a = (g * jax.nn.sigmoid(g) * u).astype(BF16)
        d = jnp.dot(a, wd_ref[sl, :], preferred_element_type=F32)
        acc = d if acc is None else acc + d
        yield acc


def _interleave(steps, stages):
    stages = list(stages)
    done = 0
    for k, value in enumerate(steps):
        while stages and done * 2 <= k:
            stages.pop(0)()
            done += 1
    for stage in stages:
        stage()
    return value


def _swiglu(hn, wg_ref, wu_ref, wd_ref, fc):
    return _interleave(_swiglu_steps(hn, wg_ref, wu_ref, wd_ref, fc), [])


def _layer0_kernel(x_ref, xnext_ref, first_ref, g1_ref, pw_ref, ps_ref, g2_ref,
                   wg_ref, wu_ref, wd_ref, o_ref, hmid_ref, hn2_ref, *, tiles_per_seq, pos0, fc, sub):
    i = pl.program_id(0)
    rows = x_ref.shape[0]
    n_sub = rows // sub
    gc = pw_ref.shape[1]
    g1 = g1_ref[...]
    slot = i % 2

    def mixer_stages(load_x, halo, pos_base):
        st = {"mix": []}

        def norm():
            st["x"] = load_x()
            st["hn"] = _rms(st["x"], g1)
            st["full"] = jnp.concatenate([_rms(halo(), g1), st["hn"]], axis=0)
            st["pos"] = lax.broadcasted_iota(jnp.int32, (sub, 1), 0) + pos_base

        def pool(g, w):
            cols = slice(g * gc, (g + 1) * gc)
            s = st["full"][:, cols]
            k = 1
            while k < w:
                s = s + pltpu.roll(s, k, 0)
                k *= 2
            inv_cnt = 1.0 / jnp.minimum(st["pos"] + 1, w).astype(F32)
            pooled = s[HALO:] * inv_cnt - st["hn"][:, cols]
            st["mix"].append(jnp.dot(pooled.astype(BF16), pw_ref[g], preferred_element_type=F32))

        def residual():
            st["hmid"] = st["x"] + jnp.concatenate(st["mix"], axis=-1) * ps_ref[...]
            st["hn2"] = _rms(st["hmid"], g2_ref[...]).astype(BF16)

        stages = [norm] + [functools.partial(pool, g, w) for g, w in enumerate(POOL_WINDOWS)] + [residual]
        return stages, st

    @pl.when(i == 0)
    def _():
        stages, st = mixer_stages(lambda: x_ref[:sub, :], lambda: first_ref[...], pos0)
        for stage in stages:
            stage()
        hmid_ref[0] = st["hmid"]
        hn2_ref[0] = st["hn2"]

    tile_in_seq = i % tiles_per_seq
    next_in_seq = (i + 1) % tiles_per_seq
    st = None
    for c in range(n_sub):
        r = slice(c * sub, (c + 1) * sub)
        if c + 1 < n_sub:
            nr = slice((c + 1) * sub, (c + 2) * sub)
            nxt_stages, nxt = mixer_stages(lambda nr=nr: x_ref[nr, :],
                                           lambda nr=nr: x_ref[nr.start - HALO:nr.start, :],
                                           tile_in_seq * rows + nr.start + pos0)
        else:
            nxt_stages, nxt = mixer_stages(
                lambda: xnext_ref[...],
                lambda: jnp.where(next_in_seq == 0, first_ref[...], x_ref[rows - HALO:, :]),
                next_in_seq * rows + pos0)

            def hand_over(nxt=nxt):
                hmid_ref[1 - slot] = nxt["hmid"]
                hn2_ref[1 - slot] = nxt["hn2"]

            nxt_stages = nxt_stages + [hand_over]
        hn2 = hn2_ref[slot] if c == 0 else st["hn2"]
        ffn = _interleave(_swiglu_steps(hn2, wg_ref, wu_ref, wd_ref, fc), nxt_stages)
        o_ref[r, :] = (hmid_ref[slot] if c == 0 else st["hmid"]) + ffn
        st = nxt


def _resident(shape, layer=None):
    if layer is None:
        return pl.BlockSpec(shape, lambda *_: (0,) * len(shape), pipeline_mode=pl.Buffered(1))
    return pl.BlockSpec((None,) + shape, lambda *_: (layer,) + (0,) * len(shape), pipeline_mode=pl.Buffered(1))


def _layer0(x, first, g1, pw, ps, g2, wg, wu, wd, *, layer, rows, tiles_per_seq, pos0, fc):
    n, d = x.shape
    f = wg.shape[2]
    sub = min(rows, SUB_ROWS)
    n_sub = rows // sub
    kern = functools.partial(_layer0_kernel, tiles_per_seq=tiles_per_seq, pos0=pos0, fc=fc, sub=sub)
    return pl.pallas_call(
        kern,
        out_shape=jax.ShapeDtypeStruct((n, d), F32),
        grid=(n // rows,),
        in_specs=[
            pl.BlockSpec((rows, d), lambda i: (i, 0)),
            pl.BlockSpec((sub, d), lambda i: (jnp.minimum((i + 1) * n_sub, n // sub - 1), 0)),
            _resident((HALO, d)),
            _resident((1, d)),
            _resident(pw.shape),
            _resident((1, d)),
            _resident((1, d)),
            _resident((d, f), layer),
            _resident((d, f), layer),
            _resident((f, d), layer),
        ],
        out_specs=pl.BlockSpec((rows, d), lambda i: (i, 0)),
        scratch_shapes=[pltpu.VMEM((2, sub, d), F32),
                        pltpu.VMEM((2, sub, d), BF16)],
        compiler_params=pltpu.CompilerParams(
            dimension_semantics=("arbitrary",), vmem_limit_bytes=V7X_VMEM_LIMIT),
        name="layer0_pool_swiglu",
    )(x, x, first, g1, pw, ps, g2, wg, wu, wd)


def _qkv_kernel(h_ref, g_ref, w_ref, o_ref):
    d = h_ref.shape[1]
    hn = _rms(h_ref[...], g_ref[...]).astype(BF16)
    for c in range(3):
        sl = slice(c * d, (c + 1) * d)
        y = jnp.dot(hn, w_ref[:, sl], preferred_element_type=F32)
        if c == 0:
            y = y * (LOG2E / math.sqrt(HEAD_DIM))
        o_ref[:, sl] = y.astype(BF16)


def _qkv(h, g, w, *, rows):
    n, d = h.shape
    return pl.pallas_call(
        _qkv_kernel,
        out_shape=jax.ShapeDtypeStruct((n, 3 * d), BF16),
        grid=(n // rows,),
        in_specs=[
            pl.BlockSpec((rows, d), lambda i: (i, 0)),
            _resident((1, d)),
            _resident((d, 3 * d)),
        ],
        out_specs=pl.BlockSpec((rows, 3 * d), lambda i: (i, 0)),
        compiler_params=pltpu.CompilerParams(
            dimension_semantics=("arbitrary",), vmem_limit_bytes=V7X_VMEM_LIMIT),
        name="qkv_proj",
    )(h, g, w)


def _bucket_table(rel):
    n = np.maximum(rel, 0)
    max_exact = N_BUCKETS // 2
    nf = np.maximum(n, max_exact).astype(np.float32)
    large = max_exact + (np.log(nf / np.float32(max_exact)) / np.float32(math.log(MAX_DISTANCE / max_exact))
                         * np.float32(N_BUCKETS - max_exact)).astype(np.int32)
    large = np.minimum(large, N_BUCKETS - 1)
    return np.where(rel < 0, -1, np.where(n < max_exact, n, large)).astype(np.int32)


def _bias_kernel(rb_ref, lv_ref, idxd_ref, idxm_ref, bd_ref, bm_ref, lam_ref, *, lambda_init):
    h = pl.program_id(0)
    far = rb_ref[N_BUCKETS - 1, h]

    def build(idx):
        out = jnp.full(idx.shape, NEG_INF, F32)
        for b in range(N_BUCKETS):
            out = jnp.where(idx == b, (rb_ref[b, h] - far) * LOG2E, out)
        return out

    near = [build(idxd_ref[e]) for e in range(2)]
    const = {True: jnp.full((BIAS_BLOCK, BIAS_BLOCK), NEG_INF, F32),
             False: jnp.zeros((BIAS_BLOCK, BIAS_BLOCK), F32)}
    blocks = bd_ref.shape[1] // BIAS_BLOCK
    for d in range(bd_ref.shape[0]):
        for a in range(blocks):
            for c in range(blocks):
                e = d * blocks + a - c
                bd_ref[d, a * BIAS_BLOCK:(a + 1) * BIAS_BLOCK, c * BIAS_BLOCK:(c + 1) * BIAS_BLOCK] = (
                    near[e] if 0 <= e < 2 else const[e < 0])
    bm_ref[...] = jnp.zeros(bm_ref.shape, F32)
    bm_ref[:BIAS_BLOCK, :] = build(idxm_ref[...])
    lv = lv_ref[...]
    e1 = jnp.exp(jnp.sum(lv[0:1] * lv[1:2], axis=-1, keepdims=True))
    e2 = jnp.exp(jnp.sum(lv[2:3] * lv[3:4], axis=-1, keepdims=True))
    lam_ref[...] = e1 - e2 + lambda_init


def _bias_tables(rel_bias, lam_vecs, *, tq, lambda_init):
    n_heads = rel_bias.shape[1]
    assert tq % BIAS_BLOCK == 0
    qi = np.arange(BIAS_BLOCK)[:, None]
    idx_d = np.stack([_bucket_table(e * BIAS_BLOCK + qi - np.arange(BIAS_BLOCK)[None, :]) for e in range(2)])
    idx_m = _bucket_table(N_META + qi - np.arange(N_META)[None, :])
    kern = functools.partial(_bias_kernel, lambda_init=lambda_init)
    return pl.pallas_call(
        kern,
        out_shape=(jax.ShapeDtypeStruct((n_heads, 2, tq, tq), F32),
                   jax.ShapeDtypeStruct((n_heads, tq, N_META), F32),
                   jax.ShapeDtypeStruct((1, 1), F32)),
        grid=(n_heads,),
        in_specs=[
            pl.BlockSpec(memory_space=pltpu.SMEM),
            pl.BlockSpec(lam_vecs.shape, lambda h: (0, 0)),
            pl.BlockSpec(idx_d.shape, lambda h: (0, 0, 0)),
            pl.BlockSpec(idx_m.shape, lambda h: (0, 0)),
        ],
        out_specs=(pl.BlockSpec((None, 2, tq, tq), lambda h: (h, 0, 0, 0)),
                   pl.BlockSpec((None, tq, N_META), lambda h: (h, 0, 0)),
                   pl.BlockSpec((1, 1), lambda h: (0, 0))),
        compiler_params=pltpu.CompilerParams(dimension_semantics=("arbitrary",)),
        name="rel_bias_tiles",
    )(rel_bias, lam_vecs, jnp.asarray(idx_d), jnp.asarray(idx_m))


def _attn_kernel(lam_ref, q_ref, k_ref, v_ref, km_ref, vm_ref, bd_ref, bm_ref, g_ref,
                 o_ref, vext_ref, qs_ref, s_ref, m_ref, acc_ref, *, tq, lambda_init, heads):
    n_q = q_ref.shape[0] // tq
    ones = jnp.ones((1, V_DIM), BF16)
    lanes = [slice(u * V_DIM, (u + 1) * V_DIM) for u in range(heads)]
    lane = lax.broadcasted_iota(jnp.int32, (tq, V_DIM), 1)

    for u in range(heads):
        vext_ref[u, :, :V_DIM] = v_ref[:, lanes[u]]
        vext_ref[u, :, V_DIM:] = jnp.broadcast_to(ones, (v_ref.shape[0], V_DIM))

    def both(b):
        return jnp.concatenate([b, b], axis=0)

    def row_max(s):
        return jnp.broadcast_to(jnp.max(s, axis=-1, keepdims=True), (s.shape[0], V_DIM))

    def wide(r, like):
        return jnp.concatenate([r] * (like.shape[1] // V_DIM), axis=1)

    def scores(u, tile):
        off = pl.multiple_of(tile * tq, tq)
        return lax.dot_general(qs_ref[u], k_ref[pl.ds(off, tq), lanes[u]], _NT, preferred_element_type=F32)

    def weighted_values(u, p, tile):
        off = pl.multiple_of(tile * tq, tq)
        return jnp.dot(p, vext_ref[u, pl.ds(off, tq), :], preferred_element_type=F32)

    def start(i):
        rows = pl.ds(pl.multiple_of(i * tq, tq), tq)
        for u in range(heads):
            q = q_ref[rows, lanes[u]]
            zero = jnp.zeros_like(q)
            qs_ref[u] = jnp.concatenate([jnp.where(lane < HEAD_DIM, q, zero),
                                         jnp.where(lane >= HEAD_DIM, q, zero)], axis=0)
            s_ref[u] = scores(u, i) + both(bd_ref[u, 0])

    def first_step(i, first_tile):
        for u in range(heads):
            s_meta = lax.dot_general(qs_ref[u], km_ref[:, lanes[u]], _NT, preferred_element_type=F32)
            if first_tile:
                s_meta = s_meta + both(bm_ref[u])
            else:
                s_next = scores(u, i - 1) + both(bd_ref[u, 1])
            s = s_ref[u]
            m = jnp.maximum(row_max(s), row_max(s_meta))
            p_meta = jnp.exp2(s_meta - m[:, :N_META]).astype(BF16)
            p = jnp.exp2(s - wide(m, s)).astype(BF16)
            m_ref[u] = m
            if not first_tile:
                s_ref[u] = s_next
            vm_ext = jnp.concatenate([vm_ref[:, lanes[u]], jnp.broadcast_to(ones, (N_META, V_DIM))], axis=-1)
            acc_ref[u] = (jnp.dot(p_meta, vm_ext, preferred_element_type=F32)
                          + weighted_values(u, p, i))

    def step(next_tile, tile):
        for u in range(heads):
            s_next = scores(u, next_tile)
            s = s_ref[u]
            m_old = m_ref[u]
            m_new = jnp.maximum(m_old, row_max(s))
            p = jnp.exp2(s - wide(m_new, s)).astype(BF16)
            m_ref[u] = m_new
            s_ref[u] = s_next
            pv = weighted_values(u, p, tile)
            acc_ref[u] = acc_ref[u] * wide(jnp.exp2(m_old - m_new), pv) + pv

    def finish(i, last_step):
        for u in range(heads):
            for c in range(tq // FINISH_ROWS):
                r1 = slice(c * FINISH_ROWS, (c + 1) * FINISH_ROWS)
                r2 = slice(tq + c * FINISH_ROWS, tq + (c + 1) * FINISH_ROWS)
                pick = lambda ref: jnp.concatenate([ref[u, r1, :], ref[u, r2, :]], axis=0)
                acc = pick(acc_ref)
                if last_step:
                    s = pick(s_ref)
                    m_old = pick(m_ref)
                    m_new = jnp.maximum(m_old, row_max(s))
                    p = jnp.exp2(s - wide(m_new, s)).astype(BF16)
                    acc = acc * wide(jnp.exp2(m_old - m_new), acc) + weighted_values(u, p, 0)
                o = acc[:, :V_DIM] / acc[:, V_DIM:]
                o = o[:FINISH_ROWS] - lam_ref[0, 0] * o[FINISH_ROWS:]
                o = _rms(o, g_ref[...]) * (1.0 - lambda_init)
                o_ref[pl.ds(pl.multiple_of(i * tq, tq) + c * FINISH_ROWS, FINISH_ROWS), lanes[u]] = o.astype(BF16)

    start(0)

    @pl.loop(0, n_q)
    def _(i):
        for first_tile in (True, False):
            @pl.when((i == 0) == first_tile)
            def _():
                first_step(i, first_tile)

        far = jnp.maximum(i - 1, 0)

        @pl.loop(0, far // 2)
        def _(j):
            t = 2 * j + 1
            step(i - t - 1, i - t)
            step(i - t - 2, i - t - 1)

        @pl.when(far % 2 == 1)
        def _():
            step(0, 1)

        for first_tile, more in ((True, n_q > 1), (False, True), (False, False)):
            @pl.when(((i == 0) == first_tile) & ((i < n_q - 1) == more))
            def _():
                finish(i, last_step=not first_tile)
                if more:
                    start(i + 1)


def _attention(qkv, qkv_meta, lam, bias_d, bias_m, subln_g, *, tq, lambda_init, heads):
    b, s, d3 = qkv.shape
    d = d3 // 3
    n_groups = d // (V_DIM * heads)
    w = V_DIM * heads
    kern = functools.partial(_attn_kernel, tq=tq, lambda_init=lambda_init, heads=heads)
    return pl.pallas_call(
        kern,
        out_shape=jax.ShapeDtypeStruct((b, s, d), BF16),
        grid=(b, n_groups),
        in_specs=[
            pl.BlockSpec(memory_space=pltpu.SMEM),
            pl.BlockSpec((None, s, w), lambda b, g: (b, 0, g)),
            pl.BlockSpec((None, s, w), lambda b, g: (b, 0, n_groups + g)),
            pl.BlockSpec((None, s, w), lambda b, g: (b, 0, 2 * n_groups + g)),
            pl.BlockSpec((N_META, w), lambda b, g: (0, n_groups + g)),
            pl.BlockSpec((N_META, w), lambda b, g: (0, 2 * n_groups + g)),
            pl.BlockSpec((heads, 2, tq, tq), lambda b, g: (g, 0, 0, 0)),
            pl.BlockSpec((heads, tq, N_META), lambda b, g: (g, 0, 0)),
            pl.BlockSpec((1, V_DIM), lambda b, g: (0, 0)),
        ],
        out_specs=pl.BlockSpec((None, s, w), lambda b, g: (b, 0, g)),
        scratch_shapes=[
            pltpu.VMEM((heads, s, 2 * V_DIM), BF16),
            pltpu.VMEM((heads, 2 * tq, V_DIM), BF16),
            pltpu.VMEM((heads, 2 * tq, tq), F32),
            pltpu.VMEM((heads, 2 * tq, V_DIM), F32),
            pltpu.VMEM((heads, 2 * tq, 2 * V_DIM), F32),
        ],
        compiler_params=pltpu.CompilerParams(
            dimension_semantics=("arbitrary", "arbitrary"),
            vmem_limit_bytes=V7X_VMEM_LIMIT),
        name="diff_attention",
    )(lam, qkv, qkv, qkv, qkv_meta, qkv_meta, bias_d, bias_m, subln_g)


def _layer1_kernel(a_ref, h_ref, wo_ref, g2_ref, wg_ref, wu_ref, wd_ref, gf_ref, o_ref, *, fc):
    sub = min(a_ref.shape[0], SUB_ROWS)
    n_sub = a_ref.shape[0] // sub

    def prepare(c):
        r = slice(c * sub, (c + 1) * sub)
        st = {"rows": r}

        def out_proj():
            st["hmid"] = h_ref[r, :] + jnp.dot(a_ref[r, :], wo_ref[...], preferred_element_type=F32)
            st["hn2"] = _rms(st["hmid"], g2_ref[...]).astype(BF16)

        return [out_proj], st

    def finalize(st, ffn):
        o_ref[st["rows"], :] = _rms(st["hmid"] + ffn, gf_ref[...])

    stages, st = prepare(0)
    stages[0]()
    pending = []
    for c in range(n_sub):
        nxt_stages, nxt = prepare(c + 1) if c + 1 < n_sub else ([], None)
        ffn = _interleave(_swiglu_steps(st["hn2"], wg_ref, wu_ref, wd_ref, fc), pending + nxt_stages)
        pending = [functools.partial(finalize, st, ffn)]
        st = nxt
    pending[0]()


def _layer1(a, h, wo, g2, wg, wu, wd, gf, *, layer, rows, fc):
    n, d = h.shape
    f = wg.shape[2]
    kern = functools.partial(_layer1_kernel, fc=fc)
    return pl.pallas_call(
        kern,
        out_shape=jax.ShapeDtypeStruct((n, d), F32),
        grid=(n // rows,),
        in_specs=[
            pl.BlockSpec((rows, d), lambda i: (i, 0)),
            pl.BlockSpec((rows, d), lambda i: (i, 0)),
            _resident((d, d)),
            _resident((1, d)),
            _resident((d, f), layer),
            _resident((d, f), layer),
            _resident((f, d), layer),
            _resident((1, d)),
        ],
        out_specs=pl.BlockSpec((rows, d), lambda i: (i, 0)),
        compiler_params=pltpu.CompilerParams(
            dimension_semantics=("arbitrary",), vmem_limit_bytes=V7X_VMEM_LIMIT),
        name="layer1_out_swiglu_norm",
    )(a, h, wo, g2, wg, wu, wd, gf)


ATTN_HEADS_PER_STEP = 2


def _pick_rows(seq, want):
    rows = min(seq, want)
    assert seq % rows == 0 and rows % HALO == 0
    return rows


def kernel(x, meta_tokens, rel_bias, mix_norm_g, ffn_norm_g, pool_w, pool_scale, attn_w_qkv, attn_w_o,
           lambda_q1, lambda_k1, lambda_q2, lambda_k2, subln_g, ffn_w_gate, ffn_w_up, ffn_w_down,
           final_norm_g):
    bsz, seq, d = x.shape
    assert meta_tokens.shape == (N_META, d) and N_META == HALO
    rows = _pick_rows(seq, 1024)
    tq = _pick_rows(seq, 512)
    fc = 256
    lambda_init = 0.8 - 0.6 * math.exp(-0.3 * 1)

    row = lambda v: v.reshape(1, -1)
    wg, wu, wd = (w.astype(BF16) for w in (ffn_w_gate, ffn_w_up, ffn_w_down))
    pw = pool_w[0].astype(BF16)
    wqkv = attn_w_qkv[0].astype(BF16)
    wo = attn_w_o[0].astype(BF16)
    xr = x.reshape(bsz * seq, d)

    l0 = functools.partial(_layer0, g1=row(mix_norm_g[0]), pw=pw, ps=row(pool_scale[0]),
                           g2=row(ffn_norm_g[0]), wg=wg, wu=wu, wd=wd, layer=0, fc=fc)
    h1 = l0(xr, meta_tokens, rows=rows, tiles_per_seq=seq // rows, pos0=N_META)
    h1_meta = l0(meta_tokens, jnp.zeros_like(meta_tokens), rows=N_META, tiles_per_seq=1, pos0=0)

    qkv = _qkv(h1, row(mix_norm_g[1]), wqkv, rows=rows)
    qkv_meta = _qkv(h1_meta, row(mix_norm_g[1]), wqkv, rows=N_META)

    lam_vecs = jnp.stack([lambda_q1[0], lambda_k1[0], lambda_q2[0], lambda_k2[0]])
    bias_d, bias_m, lam = _bias_tables(rel_bias, lam_vecs, tq=tq, lambda_init=lambda_init)
    attn = _attention(qkv.reshape(bsz, seq, 3 * d), qkv_meta, lam, bias_d, bias_m,
                      row(subln_g[0]), tq=tq, lambda_init=lambda_init, heads=ATTN_HEADS_PER_STEP)

    out = _layer1(attn.reshape(bsz * seq, d), h1, wo, row(ffn_norm_g[1]), wg, wu, wd,
                  row(final_norm_g), layer=1, rows=rows, fc=fc)
    return out.reshape(bsz, seq, d)
```
